```python
import jax, jax.numpy as jnp
from jax import lax
import numpy as np

D_MODEL = 2048
BATCH = 2
SEQ = 4096
DEPTH = 4
DEC_BATCH = 8
DEC_SEQ = 1
PAST_LEN = 16384
PAGE_SIZE = 128

N_MIXERS = 2
N_A_LAYERS = (DEPTH + N_MIXERS - 1) // N_MIXERS
N_B_LAYERS = DEPTH // N_MIXERS
HEAD_DIM = 128
SB_HEADS = D_MODEL // HEAD_DIM
SB_BLOCK = 128
SB_BIAS_INIT = -6.0
DIL_WINDOWS = (128, 512, 2048)
DIL_RATES = (1, 4, 16)
N_DIL_GROUPS = 3
DIL_HEADS = D_MODEL // (2 * HEAD_DIM)
NUM_BUCKETS = 32
BUCKET_MAX_DIST = 2048
FFN_HIDDEN = ((8 * D_MODEL + 3 * 256 - 1) // (3 * 256)) * 256
PLE_DIM = 256
RMS_EPS = 1e-6

kernel_name = 'stickbreak_dilated_hybrid_step'


def rmsnorm(x, g):
    x32 = x.astype(jnp.float32)
    y = x32 * lax.rsqrt(jnp.mean(x32 * x32, axis=-1, keepdims=True) + RMS_EPS)
    return (y * g.astype(jnp.float32)).astype(x.dtype)


def t5_bucket(dist):
    dist = np.asarray(dist, np.int64)
    max_exact = NUM_BUCKETS // 2
    large = max_exact + (np.log(np.maximum(dist, 1) / max_exact) / np.log(BUCKET_MAX_DIST / max_exact) * (NUM_BUCKETS - max_exact)).astype(np.int64)
    large = np.minimum(large, NUM_BUCKETS - 1)
    return np.where(dist < max_exact, dist, large).astype(np.int32)


def dil_bias(rel_bias, g):
    d = DIL_RATES[g]
    steps = DIL_WINDOWS[g] // d
    buckets = t5_bucket(d * np.arange(steps + 1))
    table = rel_bias[g * DIL_HEADS:(g + 1) * DIL_HEADS]
    return table[:, buckets].astype(jnp.float32)


def stick_breaking_core(q, k, v, b, q_pos, k_pos):
    z = jnp.einsum('bqhd,bkhd->bhqk', q, k, preferred_element_type=jnp.float32) * (HEAD_DIM ** -0.5)
    z = z + b.astype(jnp.float32)[None, :, None, None]
    mask = k_pos[None, :] < q_pos[:, None]
    log_beta = jax.nn.log_sigmoid(z)
    log_keep = jnp.where(mask, log_beta - z, 0.0)
    later = lax.cumsum(log_keep, axis=3, reverse=True) - log_keep
    attn = jnp.where(mask, jnp.exp(log_beta + later), 0.0)
    return jnp.einsum('bhqk,bkhd->bqhd', attn, v.astype(jnp.float32))


def sb_project(xn, w_qkv):
    B, S, _ = xn.shape
    qkv = (xn @ w_qkv).reshape(B, S, 3, SB_HEADS, HEAD_DIM)
    return qkv[:, :, 0], qkv[:, :, 1], qkv[:, :, 2]


def sb_prompt(xn, w_qkv, w_o, b):
    B, S, _ = xn.shape
    q, k, v = sb_project(xn, w_qkv)
    nb = S // SB_BLOCK
    q_blocks = q.reshape(B, nb, SB_BLOCK, SB_HEADS, HEAD_DIM).transpose(1, 0, 2, 3, 4)
    k_pos = jnp.arange(S)

    def block(args):
        qb, start = args
        return stick_breaking_core(qb, k, v, b, start + jnp.arange(SB_BLOCK), k_pos)

    o = lax.map(block, (q_blocks, jnp.arange(nb) * SB_BLOCK))
    o = o.transpose(1, 0, 2, 3, 4).reshape(B, S, SB_HEADS * HEAD_DIM).astype(xn.dtype)
    return o @ w_o, k, v


def sb_decode(xn, cache_k, cache_v, page_table, w_qkv, w_o, b):
    DB, T, _ = xn.shape
    q, k, v = sb_project(xn, w_qkv)
    past_k = cache_k[page_table].reshape(DB, -1, SB_HEADS, HEAD_DIM)
    past_v = cache_v[page_table].reshape(DB, -1, SB_HEADS, HEAD_DIM)
    L = past_k.shape[1]
    k_all = jnp.concatenate([past_k.astype(k.dtype), k], axis=1)
    v_all = jnp.concatenate([past_v.astype(v.dtype), v], axis=1)
    o = stick_breaking_core(q, k_all, v_all, b, L + jnp.arange(T), jnp.arange(L + T))
    o = o.reshape(DB, T, SB_HEADS * HEAD_DIM).astype(xn.dtype)
    return o @ w_o, k, v


def dil_project(xn, w_qkv, g_qn, g_kn):
    B, S, _ = xn.shape
    qkv = (xn @ w_qkv).reshape(B, S, N_DIL_GROUPS, 3, DIL_HEADS, HEAD_DIM)
    q = rmsnorm(qkv[:, :, :, 0], g_qn[:, None, :])
    k = rmsnorm(qkv[:, :, :, 1], g_kn[:, None, :])
    return q, k, qkv[:, :, :, 2]


def dil_group_prompt(q, k, v, bias, d, steps):
    B, S, H, D = q.shape
    n = S // d
    nb = -(-n // steps)
    pad = nb * steps - n

    def to_blocks(t):
        t = t.reshape(B, n, d, H, D).transpose(0, 2, 1, 3, 4)
        t = jnp.pad(t, ((0, 0), (0, 0), (0, pad), (0, 0), (0, 0)))
        return t.reshape(B, d, nb, steps, H, D)

    def with_prev(t):
        prev = jnp.pad(t, ((0, 0), (0, 0), (1, 0), (0, 0), (0, 0), (0, 0)))[:, :, :-1]
        return jnp.concatenate([prev, t], axis=3)

    qb = to_blocks(q)
    kc = with_prev(to_blocks(k))
    vc = with_prev(to_blocks(v))
    s = jnp.einsum('brnqhd,brnkhd->brnhqk', qb, kc, preferred_element_type=jnp.float32) * (HEAD_DIM ** -0.5)
    m = np.arange(steps)[:, None] + steps - np.arange(2 * steps)[None, :]
    band = (m >= 0) & (m <= steps)
    key_idx = np.arange(nb)[:, None] * steps + np.arange(2 * steps)[None, :] - steps
    valid = band[None] & (key_idx >= 0)[:, None, :]
    s = s + bias[:, np.clip(m, 0, steps)]
    s = jnp.where(valid[:, None], s, -jnp.inf)
    lse = jax.nn.logsumexp(s, axis=-1)
    p = jnp.exp(s - lse[..., None])
    o = jnp.einsum('brnhqk,brnkhd->brnqhd', p, vc.astype(jnp.float32))
    o = o.reshape(B, d, nb * steps, H, D)[:, :, :n].transpose(0, 2, 1, 3, 4).reshape(B, S, H, D)
    lse = lse.transpose(0, 1, 2, 4, 3).reshape(B, d, nb * steps, H)[:, :, :n].transpose(0, 2, 1, 3).reshape(B, S, H)
    return o, lse


def dil_group_decode(q, ext_k, ext_v, bias, d, steps):
    T = q.shape[1]
    L = ext_k.shape[1] - T
    idx = L + np.arange(T)[:, None] - d * np.arange(steps + 1)[None, :]
    valid = idx >= 0
    idx_c = np.maximum(idx, 0)
    kg = ext_k[:, idx_c]
    vg = ext_v[:, idx_c]
    s = jnp.einsum('bthd,btmhd->bhtm', q, kg, preferred_element_type=jnp.float32) * (HEAD_DIM ** -0.5)
    s = jnp.where(valid, s + bias[:, None, :], -jnp.inf)
    lse = jax.nn.logsumexp(s, axis=-1)
    p = jnp.exp(s - lse[..., None])
    o = jnp.einsum('bhtm,btmhd->bthd', p, vg.astype(jnp.float32))
    return o, lse.transpose(0, 2, 1)


def merge_groups(outs, lses):
    w = jax.nn.softmax(jnp.stack(lses), axis=0)
    return jnp.sum(w[..., None] * jnp.stack(outs), axis=0)


def dil_prompt(xn, w_qkv, w_o, g_qn, g_kn, rel_bias):
    B, S, _ = xn.shape
    q, k, v = dil_project(xn, w_qkv, g_qn, g_kn)
    outs, lses, tails = [], [], []
    for g in range(N_DIL_GROUPS):
        d = DIL_RATES[g]
        o, l = dil_group_prompt(q[:, :, g], k[:, :, g], v[:, :, g], dil_bias(rel_bias, g), d, DIL_WINDOWS[g] // d)
        outs.append(o)
        lses.append(l)
        L = min(DIL_WINDOWS[g], S)
        tails.append(jnp.stack([k[:, S - L:, g], v[:, S - L:, g]], axis=2))
    o = merge_groups(outs, lses).reshape(B, S, DIL_HEADS * HEAD_DIM).astype(xn.dtype)
    return o @ w_o, tails


def dil_decode(xn, bufs, w_qkv, w_o, g_qn, g_kn, rel_bias):
    DB, T, _ = xn.shape
    q, k, v = dil_project(xn, w_qkv, g_qn, g_kn)
    outs, lses, new_bufs = [], [], []
    for g in range(N_DIL_GROUPS):
        d = DIL_RATES[g]
        buf = bufs[g]
        L = buf.shape[1]
        ext_k = jnp.concatenate([buf[:, :, 0].astype(k.dtype), k[:, :, g]], axis=1)
        ext_v = jnp.concatenate([buf[:, :, 1].astype(v.dtype), v[:, :, g]], axis=1)
        o, l = dil_group_decode(q[:, :, g], ext_k, ext_v, dil_bias(rel_bias, g), d, DIL_WINDOWS[g] // d)
        outs.append(o)
        lses.append(l)
        new_bufs.append(jnp.stack([ext_k[:, -L:], ext_v[:, -L:]], axis=2))
    o = merge_groups(outs, lses).reshape(DB, T, DIL_HEADS * HEAD_DIM).astype(xn.dtype)
    return o @ w_o, new_bufs


def ffn_and_ple(h, p, g_ffn, w_gate, w_up, w_down, g_ple, w_pp, w_pg):
    f = rmsnorm(h, g_ffn)
    h = h + (jax.nn.silu(f @ w_gate) * (f @ w_up)) @ w_down
    gate = jax.nn.sigmoid(rmsnorm(h, g_ple) @ w_pg)
    return h + gate * (p.astype(h.dtype) @ w_pp)


def setup_inputs(seed: int = 0) -> dict:
    key = jax.random.key(seed)
    ks = jax.random.split(key, 32)
    f32 = jnp.float32
    n_pages = PAST_LEN // PAGE_SIZE
    n_used = DEC_BATCH * n_pages
    n_pool = n_used + n_used // 4
    sb_w = SB_HEADS * HEAD_DIM
    dil_w = DIL_HEADS * HEAD_DIM

    def nrm(k, shape, scale=1.0):
        return jax.random.normal(k, shape, f32) * scale

    def gain(k, shape):
        return 1.0 + 0.05 * jax.random.normal(k, shape, f32)

    def win_shape(g):
        return (N_B_LAYERS, DEC_BATCH, min(DIL_WINDOWS[g], PAST_LEN), 2, DIL_HEADS, HEAD_DIM)

    return {
        'x_prompt': nrm(ks[0], (BATCH, SEQ, D_MODEL)),
        'x_sample': nrm(ks[1], (DEC_BATCH, DEC_SEQ, D_MODEL)),
        'cache_sb_k': nrm(ks[2], (N_A_LAYERS, n_pool, PAGE_SIZE, SB_HEADS, HEAD_DIM)),
        'cache_sb_v': nrm(ks[3], (N_A_LAYERS, n_pool, PAGE_SIZE, SB_HEADS, HEAD_DIM)),
        'state_win0_kv': nrm(ks[4], win_shape(0)),
        'state_win1_kv': nrm(ks[5], win_shape(1)),
        'state_win2_kv': nrm(ks[6], win_shape(2)),
        'page_table': jax.random.permutation(ks[7], n_pool)[:n_used].reshape(DEC_BATCH, n_pages).astype(jnp.int32),
        'p_prompt': nrm(ks[8], (DEPTH, BATCH, SEQ, PLE_DIM)),
        'p_sample': nrm(ks[9], (DEPTH, DEC_BATCH, DEC_SEQ, PLE_DIM)),
        'g_mix': gain(ks[10], (DEPTH, D_MODEL)),
        'g_ffn': gain(ks[11], (DEPTH, D_MODEL)),
        'g_ple': gain(ks[12], (DEPTH, D_MODEL)),
        'w_qkv_sb': nrm(ks[13], (N_A_LAYERS, D_MODEL, 3 * sb_w), D_MODEL ** -0.5),
        'w_o_sb': nrm(ks[14], (N_A_LAYERS, sb_w, D_MODEL), sb_w ** -0.5),
        'b_sb': SB_BIAS_INIT + 0.1 * jax.random.normal(ks[25], (N_A_LAYERS, SB_HEADS), f32),
        'w_qkv_dil': nrm(ks[15], (N_B_LAYERS, D_MODEL, N_DIL_GROUPS * 3 * dil_w), D_MODEL ** -0.5),
        'w_o_dil': nrm(ks[16], (N_B_LAYERS, dil_w, D_MODEL), dil_w ** -0.5),
        'g_qnorm_dil': gain(ks[17], (N_B_LAYERS, N_DIL_GROUPS, HEAD_DIM)),
        'g_knorm_dil': gain(ks[18], (N_B_LAYERS, N_DIL_GROUPS, HEAD_DIM)),
        'rel_bias': nrm(ks[19], (N_DIL_GROUPS * DIL_HEADS, NUM_BUCKETS), 0.5),
        'w_ffn_gate': nrm(ks[20], (DEPTH, D_MODEL, FFN_HIDDEN), D_MODEL ** -0.5),
        'w_ffn_up': nrm(ks[21], (DEPTH, D_MODEL, FFN_HIDDEN), D_MODEL ** -0.5),
        'w_ffn_down': nrm(ks[22], (DEPTH, FFN_HIDDEN, D_MODEL), FFN_HIDDEN ** -0.5),
        'w_ple_proj': nrm(ks[23], (DEPTH, PLE_DIM, D_MODEL), PLE_DIM ** -0.5),
        'w_ple_gate': nrm(ks[24], (DEPTH, D_MODEL, D_MODEL), D_MODEL ** -0.5),
    }


def reference(x_prompt, x_sample, cache_sb_k, cache_sb_v, state_win0_kv, state_win1_kv, state_win2_kv, page_table, p_prompt, p_sample, g_mix, g_ffn, g_ple, w_qkv_sb, w_o_sb, b_sb, w_qkv_dil, w_o_dil, g_qnorm_dil, g_knorm_dil, rel_bias, w_ffn_gate, w_ffn_up, w_ffn_down, w_ple_proj, w_ple_gate):
    hp, hs = x_prompt, x_sample
    win_bufs = (state_win0_kv, state_win1_kv, state_win2_kv)
    sb_kp, sb_vp, sb_ks, sb_vs = [], [], [], []
    win_p = [[] for _ in range(N_DIL_GROUPS)]
    win_s = [[] for _ in range(N_DIL_GROUPS)]
    for i in range(DEPTH):
        j = i // N_MIXERS
        if i % N_MIXERS == 0:
            mp, kp, vp = sb_prompt(rmsnorm(hp, g_mix[i]), w_qkv_sb[j], w_o_sb[j], b_sb[j])
            ms, ks_, vs_ = sb_decode(rmsnorm(hs, g_mix[i]), cache_sb_k[j], cache_sb_v[j], page_table, w_qkv_sb[j], w_o_sb[j], b_sb[j])
            sb_kp.append(kp)
            sb_vp.append(vp)
            sb_ks.append(ks_)
            sb_vs.append(vs_)
        else:
            mp, tails = dil_prompt(rmsnorm(hp, g_mix[i]), w_qkv_dil[j], w_o_dil[j], g_qnorm_dil[j], g_knorm_dil[j], rel_bias)
            ms, new_bufs = dil_decode(rmsnorm(hs, g_mix[i]), [b[j] for b in win_bufs], w_qkv_dil[j], w_o_dil[j], g_qnorm_dil[j], g_knorm_dil[j], rel_bias)
            for g in range(N_DIL_GROUPS):
                win_p[g].append(tails[g])
                win_s[g].append(new_bufs[g])
        hp = hp + mp
        hs = hs + ms
        hp = ffn_and_ple(hp, p_prompt[i], g_ffn[i], w_ffn_gate[i], w_ffn_up[i], w_ffn_down[i], g_ple[i], w_ple_proj[i], w_ple_gate[i])
        hs = ffn_and_ple(hs, p_sample[i], g_ffn[i], w_ffn_gate[i], w_ffn_up[i], w_ffn_down[i], g_ple[i], w_ple_proj[i], w_ple_gate[i])
    sb_k_prompt = jnp.stack(sb_kp)
    sb_v_prompt = jnp.stack(sb_vp)
    sb_k_sample = jnp.stack(sb_ks)
    sb_v_sample = jnp.stack(sb_vs)
    win0_kv_prompt = jnp.stack(win_p[0])
    win1_kv_prompt = jnp.stack(win_p[1])
    win2_kv_prompt = jnp.stack(win_p[2])
    win0_kv_sample = jnp.stack(win_s[0])
    win1_kv_sample = jnp.stack(win_s[1])
    win2_kv_sample = jnp.stack(win_s[2])
    return (hp, hs, sb_k_prompt, sb_v_prompt, sb_k_sample, sb_v_sample, win0_kv_prompt, win1_kv_prompt, win2_kv_prompt, win0_kv_sample, win1_kv_sample, win2_kv_sample)
```

```python
import functools

import numpy as np
import jax
import jax.numpy as jnp
from jax import lax
from jax.experimental import pallas as pl
from jax.experimental.pallas import tpu as pltpu

F32 = jnp.float32
BF16 = jnp.bfloat16

HEAD_DIM = 128
RMS_EPS = 1e-6
ATTN_SCALE = HEAD_DIM ** -0.5
DIL_WINDOWS = (128, 512, 2048)
DIL_RATES = (1, 4, 16)
DIL_STEPS = 128
N_DIL_GROUPS = 3
NUM_BUCKETS = 32
BUCKET_MAX_DIST = 2048
MASK_VALUE = -1e30
SAMPLE_ROWS = 16

V7X_VMEM_BYTES = 64 * 2**20
VMEM_CEILING = V7X_VMEM_BYTES - 8 * 2**20

NT_DIMS = (((1,), (1,)), ((), ()))


def _compiler_params(n_grid, pipelined_bytes, resident_bytes):
    need = 2 * pipelined_bytes + resident_bytes + 4 * 2**20
    return pltpu.CompilerParams(
        dimension_semantics=("arbitrary",) * n_grid,
        vmem_limit_bytes=int(min(max(need, 16 * 2**20), VMEM_CEILING)))


def _nbytes(shape, dtype):
    return int(np.prod(shape)) * jnp.dtype(dtype).itemsize


def _pick(total, preferred):
    b = min(total, preferred)
    while total % b:
        b //= 2
    return b


def _rms_to_bf16(x_ref, g_ref, xn_ref):
    x = x_ref[...]
    ms = jnp.mean(x * x, axis=-1, keepdims=True)
    xn_ref[...] = (x * lax.rsqrt(ms + RMS_EPS) * g_ref[...]).astype(BF16)


def _norm_matmul_kernel(x_ref, g_ref, w_ref, o_ref, xn_ref):
    @pl.when(pl.program_id(1) == 0)
    def _():
        _rms_to_bf16(x_ref, g_ref, xn_ref)

    o_ref[...] = jnp.dot(xn_ref[...], w_ref[...], preferred_element_type=F32)


def _norm_matmul_headnorm_kernel(x_ref, g_ref, w_ref, hg_ref, o_ref, xn_ref):
    n = pl.program_id(1)

    @pl.when(n == 0)
    def _():
        _rms_to_bf16(x_ref, g_ref, xn_ref)

    y = jnp.dot(xn_ref[...], w_ref[...], preferred_element_type=F32)
    is_v = (n % 3) == 2

    @pl.when(is_v)
    def _():
        o_ref[...] = y

    @pl.when(jnp.logical_not(is_v))
    def _():
        for h in range(y.shape[1] // HEAD_DIM):
            sl = slice(h * HEAD_DIM, (h + 1) * HEAD_DIM)
            yh = y[:, sl]
            ms = jnp.mean(yh * yh, axis=-1, keepdims=True)
            o_ref[:, sl] = yh * lax.rsqrt(ms + RMS_EPS) * hg_ref[:, sl]


def _norm_matmul(x, g, w, *, slab_w, head_gain=None, bm_pref=1024, bn_pref=512):
    M, K = x.shape
    N = w.shape[1]
    bm = _pick(M, bm_pref)
    bn = slab_w if head_gain is not None else _pick(slab_w, bn_pref)
    per = slab_w // bn
    grid = (M // bm, N // bn)
    in_specs = [
        pl.BlockSpec((bm, K), lambda i, n: (i, 0)),
        pl.BlockSpec((1, K), lambda i, n: (0, 0)),
        pl.BlockSpec((K, bn), lambda i, n: (0, n)),
    ]
    args = [x, g.reshape(1, K), w]
    if head_gain is not None:
        in_specs.append(pl.BlockSpec((None, 1, slab_w), lambda i, n: (n, 0, 0)))
        args.append(head_gain)
        body = _norm_matmul_headnorm_kernel
    else:
        body = _norm_matmul_kernel
    pipelined = _nbytes((bm, K), F32) + _nbytes((K, bn), BF16) + _nbytes((bm, bn), F32)
    resident = _nbytes((bm, K), BF16) + 2 * _nbytes((bm, bn), F32)
    return pl.pallas_call(
        body,
        grid=grid,
        in_specs=in_specs,
        out_specs=pl.BlockSpec((None, bm, bn), lambda i, n: (n // per, i, n % per)),
        out_shape=jax.ShapeDtypeStruct((N // slab_w, M, slab_w), F32),
        scratch_shapes=[pltpu.VMEM((bm, K), BF16)],
        compiler_params=_compiler_params(2, pipelined, resident),
    )(*args)


def _matmul_res_kernel(x_ref, w_ref, r_ref, o_ref):
    o_ref[...] = r_ref[...] + jnp.dot(x_ref[...].astype(BF16), w_ref[...], preferred_element_type=F32)


def _matmul_res(x, w, res, *, bm_pref=512, bn_pref=512):
    M, K = x.shape
    N = w.shape[1]
    bm = _pick(M, bm_pref)
    bn = _pick(N, bn_pref)
    pipelined = _nbytes((bm, K), x.dtype) + _nbytes((K, bn), BF16) + 2 * _nbytes((bm, bn), F32)
    resident = _nbytes((bm, K), BF16) + _nbytes((bm, bn), F32)
    return pl.pallas_call(
        _matmul_res_kernel,
        grid=(M // bm, N // bn),
        in_specs=[
            pl.BlockSpec((bm, K), lambda i, n: (i, 0)),
            pl.BlockSpec((K, bn), lambda i, n: (0, n)),
            pl.BlockSpec((bm, bn), lambda i, n: (i, n)),
        ],
        out_specs=pl.BlockSpec((bm, bn), lambda i, n: (i, n)),
        out_shape=jax.ShapeDtypeStruct((M, N), F32),
        compiler_params=_compiler_params(2, pipelined, resident),
    )(x, w, res)


def _ffn_up_kernel(x_ref, g_ref, wg_ref, wu_ref, o_ref, xn_ref):
    @pl.when(pl.program_id(1) == 0)
    def _():
        _rms_to_bf16(x_ref, g_ref, xn_ref)

    xn = xn_ref[...]
    a = jnp.dot(xn, wg_ref[...], preferred_element_type=F32)
    b = jnp.dot(xn, wu_ref[...], preferred_element_type=F32)
    o_ref[...] = (a * jax.nn.sigmoid(a) * b).astype(o_ref.dtype)


def _ffn_up(x, g, wg, wu, *, bm_pref=1024, bn_pref=512):
    M, K = x.shape
    N = wg.shape[1]
    bm = _pick(M, bm_pref)
    bn = _pick(N, bn_pref)
    pipelined = _nbytes((bm, K), F32) + 2 * _nbytes((K, bn), BF16) + _nbytes((bm, bn), BF16)
    resident = _nbytes((bm, K), BF16) + 4 * _nbytes((bm, bn), F32)
    return pl.pallas_call(
        _ffn_up_kernel,
        grid=(M // bm, N // bn),
        in_specs=[
            pl.BlockSpec((bm, K), lambda i, n: (i, 0)),
            pl.BlockSpec((1, K), lambda i, n: (0, 0)),
            pl.BlockSpec((K, bn), lambda i, n: (0, n)),
            pl.BlockSpec((K, bn), lambda i, n: (0, n)),
        ],
        out_specs=pl.BlockSpec((bm, bn), lambda i, n: (i, n)),
        out_shape=jax.ShapeDtypeStruct((M, N), BF16),
        scratch_shapes=[pltpu.VMEM((bm, K), BF16)],
        compiler_params=_compiler_params(2, pipelined, resident),
    )(x, g.reshape(1, K), wg, wu)


def _ple_kernel(x_ref, g_ref, wg_ref, p_ref, wp_ref, r_ref, o_ref, xn_ref):
    @pl.when(pl.program_id(1) == 0)
    def _():
        _rms_to_bf16(x_ref, g_ref, xn_ref)

    a = jnp.dot(xn_ref[...], wg_ref[...], preferred_element_type=F32)
    c = jnp.dot(p_ref[...].astype(BF16), wp_ref[...], preferred_element_type=F32)
    o_ref[...] = r_ref[...] + jax.nn.sigmoid(a) * c


def _ple(x, g, w_gate, p, w_proj, *, bm_pref=1024, bn_pref=512):
    M, K = x.shape
    N = w_gate.shape[1]
    P = p.shape[1]
    bm = _pick(M, bm_pref)
    bn = _pick(N, bn_pref)
    pipelined = (_nbytes((bm, K), F32) + _nbytes((K, bn), BF16) + _nbytes((bm, P), F32)
                 + _nbytes((P, bn), BF16) + 2 * _nbytes((bm, bn), F32))
    resident = _nbytes((bm, K), BF16) + 3 * _nbytes((bm, bn), F32)
    return pl.pallas_call(
        _ple_kernel,
        grid=(M // bm, N // bn),
        in_specs=[
            pl.BlockSpec((bm, K), lambda i, n: (i, 0)),
            pl.BlockSpec((1, K), lambda i, n: (0, 0)),
            pl.BlockSpec((K, bn), lambda i, n: (0, n)),
            pl.BlockSpec((bm, P), lambda i, n: (i, 0)),
            pl.BlockSpec((P, bn), lambda i, n: (0, n)),
            pl.BlockSpec((bm, bn), lambda i, n: (i, n)),
        ],
        out_specs=pl.BlockSpec((bm, bn), lambda i, n: (i, n)),
        out_shape=jax.ShapeDtypeStruct((M, N), F32),
        scratch_shapes=[pltpu.VMEM((bm, K), BF16)],
        compiler_params=_compiler_params(2, pipelined, resident),
    )(x, g.reshape(1, K), w_gate, p, w_proj, x)


def _stick_breaking_weights(z, tri, carry, causal):
    sp = jnp.log1p(jnp.exp(-jnp.abs(z)))
    log_beta = jnp.minimum(z, 0.0) - sp
    log_keep = -jnp.maximum(z, 0.0) - sp
    if causal is not None:
        log_keep = jnp.where(causal, log_keep, 0.0)
    hi = log_keep.astype(BF16)
    lo = (log_keep - hi.astype(F32)).astype(BF16)
    later = (jnp.dot(hi, tri, preferred_element_type=F32)
             + jnp.dot(lo, tri, preferred_element_type=F32))
    w = jnp.exp(log_beta + later + carry)
    if causal is not None:
        w = jnp.where(causal, w, 0.0)
    return w, jnp.sum(log_keep, axis=1, keepdims=True)


def _strict_lower_ones(n):
    row = lax.broadcasted_iota(jnp.int32, (n, n), 0)
    col = lax.broadcasted_iota(jnp.int32, (n, n), 1)
    return jnp.where(row > col, 1.0, 0.0).astype(BF16), col < row


def _sb_attn_kernel(bias_ref, q_ref, k_ref, v_ref, o_ref, kb_ref, vb_ref, acc_ref, car_ref, *, tq, heads):
    hp = pl.program_id(1)
    i = pl.program_id(2)

    @pl.when(i == 0)
    def _():
        kb_ref[...] = k_ref[...].astype(BF16)
        vb_ref[...] = v_ref[...].astype(BF16)

    tri, causal = _strict_lower_ones(tq)
    acc_ref[...] = jnp.zeros_like(acc_ref)
    car_ref[...] = jnp.zeros_like(car_ref)

    def visit(j, mask):
        koff = pl.multiple_of(j * tq, tq)
        for g in range(heads):
            sl = slice(g * HEAD_DIM, (g + 1) * HEAD_DIM)
            q = q_ref[:, sl].astype(BF16)
            kj = kb_ref[pl.ds(koff, tq), sl]
            vj = vb_ref[pl.ds(koff, tq), sl]
            z = lax.dot_general(q, kj, NT_DIMS, preferred_element_type=F32) * ATTN_SCALE
            z = z + bias_ref[hp * heads + g]
            carry = car_ref[:, sl]
            w, row_keep = _stick_breaking_weights(z, tri, jnp.tile(carry, (1, tq // HEAD_DIM)), mask)
            acc_ref[:, sl] += jnp.dot(w.astype(BF16), vj, preferred_element_type=F32)
            car_ref[:, sl] = carry + jnp.broadcast_to(row_keep, carry.shape)

    visit(i, causal)

    def body(jj, c):
        visit(i - 1 - jj, None)
        return c

    lax.fori_loop(0, i, body, 0)
    o_ref[...] = acc_ref[...].astype(o_ref.dtype)


def _sb_attention(qkv, bias, B, S, *, tq_pref=256, heads=2):
    _, M, W = qkv.shape
    tq = _pick(S, tq_pref)
    hw = heads * HEAD_DIM
    qkv4 = qkv.reshape(3, B, S, W)
    pipelined = _nbytes((tq, hw), F32) + 2 * _nbytes((S, hw), F32) + _nbytes((tq, hw), BF16)
    resident = 2 * _nbytes((S, hw), BF16) + 2 * _nbytes((tq, hw), F32) + 8 * _nbytes((tq, tq), F32)
    out = pl.pallas_call(
        functools.partial(_sb_attn_kernel, tq=tq, heads=heads),
        grid=(B, W // hw, S // tq),
        in_specs=[
            pl.BlockSpec(memory_space=pltpu.SMEM),
            pl.BlockSpec((None, None, tq, hw), lambda b, h, i: (0, b, i, h)),
            pl.BlockSpec((None, None, S, hw), lambda b, h, i: (1, b, 0, h)),
            pl.BlockSpec((None, None, S, hw), lambda b, h, i: (2, b, 0, h)),
        ],
        out_specs=pl.BlockSpec((None, tq, hw), lambda b, h, i: (b, i, h)),
        out_shape=jax.ShapeDtypeStruct((B, S, W), BF16),
        scratch_shapes=[
            pltpu.VMEM((S, hw), BF16),
            pltpu.VMEM((S, hw), BF16),
            pltpu.VMEM((tq, hw), F32),
            pltpu.VMEM((tq, hw), F32),
        ],
        compiler_params=_compiler_params(3, pipelined, resident),
    )(bias, qkv4, qkv4, qkv4)
    return out.reshape(M, W)


def _sb_decode_kernel(pt_ref, q_ref, bias_ref, *refs, pages):
    del pt_ref
    k_refs, v_refs = refs[:pages], refs[pages:2 * pages]
    o_ref, qm_ref, acc_ref, car_ref = refs[2 * pages:]
    p = pl.program_id(1)
    H, W = qm_ref.shape
    page = k_refs[0].shape[0]
    own_head = (lax.broadcasted_iota(jnp.int32, (H, W), 0)
                == lax.broadcasted_iota(jnp.int32, (H, W), 1) // HEAD_DIM)

    @pl.when(p == 0)
    def _():
        qm_ref[...] = jnp.where(own_head, jnp.broadcast_to(q_ref[...], (H, W)), 0.0).astype(BF16)
        acc_ref[...] = jnp.zeros_like(acc_ref)
        car_ref[...] = jnp.zeros_like(car_ref)

    tri, _ = _strict_lower_ones(page)
    for s in range(pages):
        kp = k_refs[s][...].astype(BF16)
        vp = v_refs[s][...].astype(BF16)
        z = lax.dot_general(qm_ref[...], kp, NT_DIMS, preferred_element_type=F32) * ATTN_SCALE
        z = z + bias_ref[...]
        carry = car_ref[...]
        w, row_keep = _stick_breaking_weights(z, tri, carry, None)
        acc_ref[...] += jnp.dot(w.astype(BF16), vp, preferred_element_type=F32)
        car_ref[...] = carry + jnp.broadcast_to(row_keep, carry.shape)

    @pl.when(p == pl.num_programs(1) - 1)
    def _():
        o_ref[...] = jnp.sum(jnp.where(own_head, acc_ref[...], 0.0), axis=0, keepdims=True)


def _sb_decode(q, cache_k, cache_v, layer, page_table, bias, *, pages_pref=4):
    DB, W = q.shape
    H = W // HEAD_DIM
    page = cache_k.shape[2]
    n_pages = page_table.shape[1]
    pages = _pick(n_pages, pages_pref)

    def page_spec(s):
        def index(b, p, pt):
            return (layer, pt[b, n_pages - 1 - (p * pages + s)], 0, 0)
        return pl.BlockSpec((None, None, page, W), index)

    pipelined = 2 * pages * _nbytes((page, W), F32) + _nbytes((1, W), F32)
    resident = (pages * 2 * _nbytes((page, W), BF16) + _nbytes((H, W), BF16)
                + 3 * _nbytes((H, W), F32) + _nbytes((H, page), F32))
    grid_spec = pltpu.PrefetchScalarGridSpec(
        num_scalar_prefetch=1,
        grid=(DB, n_pages // pages),
        in_specs=[
            pl.BlockSpec((None, 1, W), lambda b, p, pt: (b, 0, 0)),
            pl.BlockSpec((H, page), lambda b, p, pt: (0, 0)),
        ] + [page_spec(s) for s in range(pages)] * 2,
        out_specs=pl.BlockSpec((None, 1, W), lambda b, p, pt: (b, 0, 0)),
        scratch_shapes=[
            pltpu.VMEM((H, W), BF16),
            pltpu.VMEM((H, W), F32),
            pltpu.VMEM((H, page), F32),
        ],
    )
    out = pl.pallas_call(
        functools.partial(_sb_decode_kernel, pages=pages),
        grid_spec=grid_spec,
        out_shape=jax.ShapeDtypeStruct((DB, 1, W), F32),
        compiler_params=_compiler_params(2, pipelined, resident),
    )(page_table, q.reshape(DB, 1, W), jnp.broadcast_to(bias[:, None], (H, page)),
      *([cache_k] * pages), *([cache_v] * pages))
    return out.reshape(DB, W)


def _t5_bucket(dist):
    dist = np.asarray(dist, np.int64)
    max_exact = NUM_BUCKETS // 2
    large = max_exact + (np.log(np.maximum(dist, 1) / max_exact) / np.log(BUCKET_MAX_DIST / max_exact)
                         * (NUM_BUCKETS - max_exact)).astype(np.int64)
    large = np.minimum(large, NUM_BUCKETS - 1)
    return np.where(dist < max_exact, dist, large).astype(np.int32)


def _dil_offset_bias(rel_bias, g, heads):
    buckets = _t5_bucket(DIL_RATES[g] * np.arange(DIL_STEPS + 1))
    return rel_bias[g * heads:(g + 1) * heads][:, buckets].astype(F32)


def _dil_prompt_bias(rel_bias, g, heads):
    u = np.arange(DIL_STEPS)[:, None]
    key = np.arange(2 * DIL_STEPS)[None, :]
    offset = u + DIL_STEPS - key
    valid = (offset >= 0) & (offset <= DIL_STEPS)
    table = _dil_offset_bias(rel_bias, g, heads)[:, np.clip(offset, 0, DIL_STEPS)]
    general = jnp.where(valid[None], table, MASK_VALUE)
    first = jnp.where((valid & (key >= DIL_STEPS))[None], table, MASK_VALUE)
    return jnp.stack([first, general])


def _dil_attn_kernel(q_ref, kc_ref, kp_ref, vc_ref, vp_ref, bias_ref, o_ref, l_ref):
    not_first = jnp.minimum(pl.program_id(2), 1)
    for h in range(q_ref.shape[1] // HEAD_DIM):
        sl = slice(h * HEAD_DIM, (h + 1) * HEAD_DIM)
        q = q_ref[:, sl].astype(BF16)
        keys = jnp.concatenate([kp_ref[:, sl], kc_ref[:, sl]], axis=0).astype(BF16)
        vals = jnp.concatenate([vp_ref[:, sl], vc_ref[:, sl]], axis=0).astype(BF16)
        s = lax.dot_general(q, keys, NT_DIMS, preferred_element_type=F32) * ATTN_SCALE
        s = s + bias_ref[not_first, h]
        m = jnp.max(s, axis=1, keepdims=True)
        p = jnp.exp(s - m)
        l = jnp.sum(p, axis=1, keepdims=True)
        o = jnp.dot(p.astype(BF16), vals, preferred_element_type=F32)
        o_ref[:, sl] = o / l
        l_ref[:, sl] = jnp.broadcast_to(m + jnp.log(l), o.shape)


def _dil_group_attention(qkv, g, bias, B, S):
    n_slabs, M, W = qkv.shape
    d = DIL_RATES[g]
    n = S // d
    nb = n // DIL_STEPS
    view = qkv.reshape(n_slabs, M // d, d * W)
    blk = (None, DIL_STEPS, W)

    def spec(slab, prev):
        if prev:
            return pl.BlockSpec(blk, lambda b, r, c: (slab, b * nb + jnp.maximum(c - 1, 0), r))
        return pl.BlockSpec(blk, lambda b, r, c: (slab, b * nb + c, r))

    out_spec = pl.BlockSpec((DIL_STEPS, W), lambda b, r, c: (b * nb + c, r))
    out_sds = jax.ShapeDtypeStruct((M // d, d * W), F32)
    pipelined = 7 * _nbytes((DIL_STEPS, W), F32)
    resident = _nbytes(bias.shape, F32) + 8 * _nbytes((DIL_STEPS, 2 * DIL_STEPS), F32)
    o, lse = pl.pallas_call(
        _dil_attn_kernel,
        grid=(B, d, nb),
        in_specs=[
            spec(3 * g, False), spec(3 * g + 1, False), spec(3 * g + 1, True),
            spec(3 * g + 2, False), spec(3 * g + 2, True),
            pl.BlockSpec(bias.shape, lambda b, r, c: (0, 0, 0, 0)),
        ],
        out_specs=[out_spec, out_spec],
        out_shape=[out_sds, out_sds],
        compiler_params=_compiler_params(3, pipelined, 2 * resident),
    )(view, view, view, view, view, bias)
    return o.reshape(M, W), lse.reshape(M, W)


def _merge_groups(outs, lses):
    m = functools.reduce(jnp.maximum, lses)
    es = [jnp.exp(l - m) for l in lses]
    den = functools.reduce(jnp.add, es)
    num = functools.reduce(jnp.add, [e * o for e, o in zip(es, outs)])
    return num / den


def _dil_merge_outproj_kernel(*refs):
    G = N_DIL_GROUPS
    o_refs, l_refs = refs[:G], refs[G:2 * G]
    w_ref, r_ref, out_ref, mg_ref = refs[2 * G:]

    @pl.when(pl.program_id(1) == 0)
    def _():
        mg_ref[...] = _merge_groups([o[...] for o in o_refs], [l[...] for l in l_refs]).astype(BF16)

    out_ref[...] = r_ref[...] + jnp.dot(mg_ref[...], w_ref[...], preferred_element_type=F32)


def _dil_merge_outproj(outs, lses, w, res, *, bm_pref=512, bn_pref=1024):
    M, K = outs[0].shape
    N = w.shape[1]
    bm = _pick(M, bm_pref)
    bn = _pick(N, bn_pref)
    row_spec = pl.BlockSpec((bm, K), lambda i, n: (i, 0))
    pipelined = 2 * N_DIL_GROUPS * _nbytes((bm, K), F32) + _nbytes((K, bn), BF16) + 2 * _nbytes((bm, bn), F32)
    resident = _nbytes((bm, K), BF16) + 6 * _nbytes((bm, K), F32)
    return pl.pallas_call(
        _dil_merge_outproj_kernel,
        grid=(M // bm, N // bn),
        in_specs=[row_spec] * (2 * N_DIL_GROUPS) + [
            pl.BlockSpec((K, bn), lambda i, n: (0, n)),
            pl.BlockSpec((bm, bn), lambda i, n: (i, n)),
        ],
        out_specs=pl.BlockSpec((bm, bn), lambda i, n: (i, n)),
        out_shape=jax.ShapeDtypeStruct((M, N), F32),
        scratch_shapes=[pltpu.VMEM((bm, K), BF16)],
        compiler_params=_compiler_params(2, pipelined, resident),
    )(*outs, *lses, w, res)


def _dil_decode_kernel(qkv_ref, w0_ref, w1_ref, w2_ref, wb_ref, sb_ref, o_ref):
    W = o_ref.shape[1]
    H = W // HEAD_DIM
    own_head = (lax.broadcasted_iota(jnp.int32, (H, W), 0)
                == lax.broadcasted_iota(jnp.int32, (H, W), 1) // HEAD_DIM)
    outs, lses = [], []
    for g, win_ref in enumerate((w0_ref, w1_ref, w2_ref)):
        q = qkv_ref[3 * g]
        k_new = qkv_ref[3 * g + 1]
        v_new = qkv_ref[3 * g + 2]
        qm = jnp.where(own_head, jnp.broadcast_to(q, (H, W)), 0.0)
        k_win = win_ref[:, :W].astype(BF16)
        v_win = win_ref[:, W:].astype(BF16)
        s_win = lax.dot_general(qm.astype(BF16), k_win, NT_DIMS, preferred_element_type=F32) * ATTN_SCALE
        s_win = s_win + wb_ref[g]
        s_new = jnp.sum(qm * k_new, axis=1, keepdims=True) * ATTN_SCALE + sb_ref[g][:, :1]
        m = jnp.maximum(jnp.max(s_win, axis=1, keepdims=True), s_new)
        p_win = jnp.exp(s_win - m)
        p_new = jnp.exp(s_new - m)
        l = jnp.sum(p_win, axis=1, keepdims=True) + p_new
        pv = jnp.dot(p_win.astype(BF16), v_win, preferred_element_type=F32) + p_new * v_new
        outs.append(jnp.sum(jnp.where(own_head, pv / l, 0.0), axis=0, keepdims=True))
        lses.append(jnp.sum(jnp.where(own_head, m + jnp.log(l), 0.0), axis=0, keepdims=True))
    o_ref[...] = _merge_groups(outs, lses)


def _dil_decode(qkv, states, layer, win_bias, self_bias):
    n_slabs, DB, W = qkv.shape
    H = W // HEAD_DIM
    views = []
    for g, st in enumerate(states):
        L = st.shape[2]
        assert L == DIL_WINDOWS[g], "window state must hold exactly one window"
        d = DIL_RATES[g]
        views.append(st.reshape(st.shape[0], DB, L // d, d * 2 * W))
    qkv_rows = jnp.transpose(qkv, (1, 0, 2)).reshape(DB, n_slabs, 1, W)
    win_spec = pl.BlockSpec((None, None, DIL_STEPS, 2 * W), lambda b: (layer, b, 0, 0))
    pipelined = 3 * _nbytes((DIL_STEPS, 2 * W), F32) + _nbytes((n_slabs, W), F32)
    resident = 3 * _nbytes((DIL_STEPS, 2 * W), F32) + 2 * _nbytes(win_bias.shape, F32)
    out = pl.pallas_call(
        _dil_decode_kernel,
        grid=(DB,),
        in_specs=[
            pl.BlockSpec((None, n_slabs, 1, W), lambda b: (b, 0, 0, 0)),
            win_spec, win_spec, win_spec,
            pl.BlockSpec(win_bias.shape, lambda b: (0, 0, 0)),
            pl.BlockSpec(self_bias.shape, lambda b: (0, 0, 0)),
        ],
        out_specs=pl.BlockSpec((None, 1, W), lambda b: (b, 0, 0)),
        out_shape=jax.ShapeDtypeStruct((DB, 1, W), F32),
        compiler_params=_compiler_params(1, pipelined, resident),
    )(qkv_rows, *views, win_bias, self_bias)
    return out.reshape(DB, W)


def _pad_rows(x, rows):
    return jnp.pad(x, ((0, rows - x.shape[0]), (0, 0)))


def kernel(x_prompt, x_sample, cache_sb_k, cache_sb_v, state_win0_kv, state_win1_kv, state_win2_kv, page_table, p_prompt, p_sample, g_mix, g_ffn, g_ple, w_qkv_sb, w_o_sb, b_sb, w_qkv_dil, w_o_dil, g_qnorm_dil, g_knorm_dil, rel_bias, w_ffn_gate, w_ffn_up, w_ffn_down, w_ple_proj, w_ple_gate):
    B, S, D = x_prompt.shape
    DB, T, _ = x_sample.shape
    assert T == 1, "decode step handles one new token per sample row"
    depth = g_mix.shape[0]
    M = B * S
    sb_heads = D // HEAD_DIM
    dil_w = w_o_dil.shape[1]
    dil_heads = dil_w // HEAD_DIM
    win_states = (state_win0_kv, state_win1_kv, state_win2_kv)
    n_pool, page = cache_sb_k.shape[1], cache_sb_k.shape[2]
    cache_k = cache_sb_k.reshape(cache_sb_k.shape[0], n_pool, page, D)
    cache_v = cache_sb_v.reshape(cache_sb_v.shape[0], n_pool, page, D)

    hp = x_prompt.reshape(M, D)
    hs = _pad_rows(x_sample.reshape(DB, D), SAMPLE_ROWS)

    prompt_bias = [_dil_prompt_bias(rel_bias, g, dil_heads) for g in range(N_DIL_GROUPS)]
    offset_bias = [_dil_offset_bias(rel_bias, g, dil_heads) for g in range(N_DIL_GROUPS)]
    win_bias = jnp.stack([ob[:, :0:-1] for ob in offset_bias])
    self_bias = jnp.stack([jnp.broadcast_to(ob[:, :1], (dil_heads, DIL_STEPS)) for ob in offset_bias])

    sb_kp, sb_vp, sb_ks, sb_vs = [], [], [], []
    win_p = [[] for _ in range(N_DIL_GROUPS)]
    win_s = [[] for _ in range(N_DIL_GROUPS)]

    for i in range(depth):
        j = i // 2
        if i % 2 == 0:
            w_qkv = w_qkv_sb[j].astype(BF16)
            w_o = w_o_sb[j].astype(BF16)
            qkv = _norm_matmul(hp, g_mix[i], w_qkv, slab_w=D)
            attn = _sb_attention(qkv, b_sb[j], B, S)
            hp = _matmul_res(attn, w_o, hp)
            sb_kp.append(qkv[1].reshape(B, S, sb_heads, HEAD_DIM))
            sb_vp.append(qkv[2].reshape(B, S, sb_heads, HEAD_DIM))

            qkv_s = _norm_matmul(hs, g_mix[i], w_qkv, slab_w=D)
            attn_s = _sb_decode(qkv_s[0, :DB], cache_k, cache_v, j, page_table, b_sb[j])
            hs = _matmul_res(_pad_rows(attn_s, SAMPLE_ROWS), w_o, hs)
            sb_ks.append(qkv_s[1, :DB].reshape(DB, 1, sb_heads, HEAD_DIM))
            sb_vs.append(qkv_s[2, :DB].reshape(DB, 1, sb_heads, HEAD_DIM))
        else:
            w_qkv = w_qkv_dil[j].astype(BF16)
            w_o = w_o_dil[j].astype(BF16)
            ones = jnp.ones((HEAD_DIM,), F32)
            head_gain = jnp.stack([jnp.tile(gain, dil_heads)
                                   for g in range(N_DIL_GROUPS)
                                   for gain in (g_qnorm_dil[j, g], g_knorm_dil[j, g], ones)])[:, None, :]
            qkv = _norm_matmul(hp, g_mix[i], w_qkv, slab_w=dil_w, head_gain=head_gain, bm_pref=512)
            outs, lses = zip(*[_dil_group_attention(qkv, g, prompt_bias[g], B, S)
                               for g in range(N_DIL_GROUPS)])
            hp = _dil_merge_outproj(outs, lses, w_o, hp)
            kv5 = qkv.reshape(N_DIL_GROUPS, 3, B, S, dil_heads, HEAD_DIM)
            for g in range(N_DIL_GROUPS):
                L = min(DIL_WINDOWS[g], S)
                win_p[g].append(jnp.stack([kv5[g, 1, :, S - L:], kv5[g, 2, :, S - L:]], axis=2))

            qkv_s = _norm_matmul(hs, g_mix[i], w_qkv, slab_w=dil_w, head_gain=head_gain)
            attn_s = _dil_decode(qkv_s[:, :DB], win_states, j, win_bias, self_bias)
            hs = _matmul_res(_pad_rows(attn_s, SAMPLE_ROWS), w_o, hs)
            kv4 = qkv_s[:, :DB].reshape(N_DIL_GROUPS, 3, DB, dil_heads, HEAD_DIM)
            for g in range(N_DIL_GROUPS):
                new_row = jnp.stack([kv4[g, 1], kv4[g, 2]], axis=1)[:, None]
                win_s[g].append(jnp.concatenate([win_states[g][j][:, 1:], new_row], axis=1))

        w_gate = w_ffn_gate[i].astype(BF16)
        w_up = w_ffn_up[i].astype(BF16)
        w_down = w_ffn_down[i].astype(BF16)
        w_pg = w_ple_gate[i].astype(BF16)
        w_pp = w_ple_proj[i].astype(BF16)
        hp = _matmul_res(_ffn_up(hp, g_ffn[i], w_gate, w_up), w_down, hp)
        hp = _ple(hp, g_ple[i], w_pg, p_prompt[i].reshape(M, -1), w_pp)
        hs = _matmul_res(_ffn_up(hs, g_ffn[i], w_gate, w_up), w_down, hs)
        hs = _ple(hs, g_ple[i], w_pg, _pad_rows(p_sample[i].reshape(DB, -1), SAMPLE_ROWS), w_pp)

    return (hp.reshape(B, S, D), hs[:DB].reshape(DB, 1, D),
            jnp.stack(sb_kp), jnp.stack(sb_vp), jnp.stack(sb_ks), jnp.stack(sb_vs),
            jnp.stack(win_p[0]), jnp.stack(win_p[1]), jnp.stack(win_p[2]),
            jnp.stack(win_s[0]), jnp.stack(win_s[1]), jnp.stack(win_s[2]))
```

```python
import functools

import numpy as np
import jax
import jax.numpy as jnp
from jax import lax
from jax.experimental import pallas as pl
from jax.experimental.pallas import tpu as pltpu

F32 = jnp.float32
BF16 = jnp.bfloat16

HEAD_DIM = 128
LANES = 128
RMS_EPS = 1e-6
ATTN_SCALE = HEAD_DIM ** -0.5
LOG2_E = float(np.log2(np.e))
DIL_WINDOWS = (128, 512, 2048)
DIL_RATES = (1, 4, 16)
DIL_STEPS = 128
N_DIL_GROUPS = 3
NUM_BUCKETS = 32
BUCKET_MAX_DIST = 2048
MASK_VALUE = -1e30
SAMPLE_ROWS = 16

V7X_VMEM_BYTES = 64 * 2**20
VMEM_CEILING = V7X_VMEM_BYTES - 8 * 2**20

NT_DIMS = (((1,), (1,)), ((), ()))


def _compiler_params(n_grid, pipelined_bytes, resident_bytes):
    need = 2 * pipelined_bytes + resident_bytes + 4 * 2**20
    return pltpu.CompilerParams(
        dimension_semantics=("arbitrary",) * n_grid,
        vmem_limit_bytes=int(min(max(need, 16 * 2**20), VMEM_CEILING)))


def _nbytes(shape, dtype):
    return int(np.prod(shape)) * jnp.dtype(dtype).itemsize


def _pick(total, preferred):
    b = min(total, preferred)
    while total % b:
        b //= 2
    return b


def _cast_kernel(w_ref, o_ref):
    o_ref[...] = w_ref[...].astype(o_ref.dtype)


def _cast_bf16(w, *, bn_pref=2048, block_bytes=4 * 2**20):
    L, K, N = w.shape
    bn = _pick(N, bn_pref)
    bk = _pick(K, block_bytes // (4 * bn))
    spec = pl.BlockSpec((None, bk, bn), lambda l, k, n: (l, k, n))
    return pl.pallas_call(
        _cast_kernel,
        grid=(L, K // bk, N // bn),
        in_specs=[spec],
        out_specs=spec,
        out_shape=jax.ShapeDtypeStruct(w.shape, BF16),
        compiler_params=_compiler_params(3, _nbytes((bk, bn), F32) + _nbytes((bk, bn), BF16), 0),
    )(w)


def _weight_spec(layer, K, bn):
    return pl.BlockSpec((None, K, bn), lambda i, n: (layer, 0, n))


def _rms_to_bf16(x_ref, g_ref, xn_ref):
    x = x_ref[...]
    ms = jnp.mean(x * x, axis=-1, keepdims=True)
    xn_ref[...] = (x * lax.rsqrt(ms + RMS_EPS) * g_ref[...]).astype(BF16)


def _qkv_kernel(x_ref, g_ref, w_ref, kv_ref, qkv_ref, xn_ref, *, per):
    n = pl.program_id(1)

    @pl.when(n == 0)
    def _():
        _rms_to_bf16(x_ref, g_ref, xn_ref)

    y = jnp.dot(xn_ref[...], w_ref[...], preferred_element_type=F32)
    qkv_ref[...] = y.astype(BF16)

    @pl.when(n >= per)
    def _():
        kv_ref[...] = y


def _sb_qkv(x, g, w, layer, *, bm_pref=1024, bn_pref=512):
    M, K = x.shape
    N = w.shape[2]
    D = N // 3
    bm = _pick(M, bm_pref)
    bn = _pick(D, bn_pref)
    per = D // bn

    def kv_index(i, n):
        return (jnp.maximum(n // per - 1, 0), i, jnp.where(n >= per, n % per, 0))

    pipelined = _nbytes((bm, K), F32) + _nbytes((K, bn), BF16) + _nbytes((bm, bn), F32) + _nbytes((bm, bn), BF16)
    resident = _nbytes((bm, K), BF16) + 2 * _nbytes((bm, bn), F32)
    return pl.pallas_call(
        functools.partial(_qkv_kernel, per=per),
        grid=(M // bm, N // bn),
        in_specs=[
            pl.BlockSpec((bm, K), lambda i, n: (i, 0)),
            pl.BlockSpec((1, K), lambda i, n: (0, 0)),
            _weight_spec(layer, K, bn),
        ],
        out_specs=[
            pl.BlockSpec((None, bm, bn), kv_index),
            pl.BlockSpec((None, bm, bn), lambda i, n: (n // per, i, n % per)),
        ],
        out_shape=[jax.ShapeDtypeStruct((2, M, D), F32), jax.ShapeDtypeStruct((3, M, D), BF16)],
        scratch_shapes=[pltpu.VMEM((bm, K), BF16)],
        compiler_params=_compiler_params(2, pipelined, resident),
    )(x, g.reshape(1, K), w)


def _dil_qkv_kernel(x_ref, g_ref, w_ref, hg_ref, o_ref, xn_ref):
    n = pl.program_id(1)

    @pl.when(n == 0)
    def _():
        _rms_to_bf16(x_ref, g_ref, xn_ref)

    y = jnp.dot(xn_ref[...], w_ref[...], preferred_element_type=F32)
    is_v = (n % 3) == 2
    heads = o_ref.shape[0]

    @pl.when(is_v)
    def _():
        for h in range(heads):
            o_ref[h] = y[:, h * HEAD_DIM:(h + 1) * HEAD_DIM]

    @pl.when(jnp.logical_not(is_v))
    def _():
        for h in range(heads):
            sl = slice(h * HEAD_DIM, (h + 1) * HEAD_DIM)
            yh = y[:, sl]
            ms = jnp.mean(yh * yh, axis=-1, keepdims=True)
            o_ref[h] = yh * lax.rsqrt(ms + RMS_EPS) * hg_ref[:, sl]


def _dil_qkv(x, g, w, layer, head_gain, *, bm_pref=512):
    M, K = x.shape
    N = w.shape[2]
    W = head_gain.shape[2]
    heads = W // HEAD_DIM
    bm = _pick(M, bm_pref)
    pipelined = _nbytes((bm, K), F32) + _nbytes((K, W), BF16) + _nbytes((bm, W), F32)
    resident = _nbytes((bm, K), BF16) + 2 * _nbytes((bm, W), F32)
    return pl.pallas_call(
        _dil_qkv_kernel,
        grid=(M // bm, N // W),
        in_specs=[
            pl.BlockSpec((bm, K), lambda i, n: (i, 0)),
            pl.BlockSpec((1, K), lambda i, n: (0, 0)),
            _weight_spec(layer, K, W),
            pl.BlockSpec((None, 1, W), lambda i, n: (n, 0, 0)),
        ],
        out_specs=pl.BlockSpec((heads, bm, HEAD_DIM), lambda i, n: (n, i, 0)),
        out_shape=jax.ShapeDtypeStruct((N // HEAD_DIM, M, HEAD_DIM), F32),
        scratch_shapes=[pltpu.VMEM((bm, K), BF16)],
        compiler_params=_compiler_params(2, pipelined, resident),
    )(x, g.reshape(1, K), w, head_gain)


def _matmul_res_kernel(x_ref, w_ref, r_ref, o_ref):
    o_ref[...] = r_ref[...] + jnp.dot(x_ref[...].astype(BF16), w_ref[...], preferred_element_type=F32)


def _matmul_res(x, w, layer, res, *, bm_pref=512, bn_pref=512):
    M, K = x.shape
    N = w.shape[2]
    bm = _pick(M, bm_pref)
    bn = _pick(N, bn_pref)
    pipelined = _nbytes((bm, K), x.dtype) + _nbytes((K, bn), BF16) + 2 * _nbytes((bm, bn), F32)
    resident = _nbytes((bm, K), BF16) + _nbytes((bm, bn), F32)
    return pl.pallas_call(
        _matmul_res_kernel,
        grid=(M // bm, N // bn),
        in_specs=[
            pl.BlockSpec((bm, K), lambda i, n: (i, 0)),
            _weight_spec(layer, K, bn),
            pl.BlockSpec((bm, bn), lambda i, n: (i, n)),
        ],
        out_specs=pl.BlockSpec((bm, bn), lambda i, n: (i, n)),
        out_shape=jax.ShapeDtypeStruct((M, N), F32),
        compiler_params=_compiler_params(2, pipelined, resident),
    )(x, w, res)


def _ffn_up_kernel(x_ref, g_ref, wg_ref, wu_ref, o_ref, xn_ref):
    @pl.when(pl.program_id(1) == 0)
    def _():
        _rms_to_bf16(x_ref, g_ref, xn_ref)

    xn = xn_ref[...]
    a = jnp.dot(xn, wg_ref[...], preferred_element_type=F32)
    b = jnp.dot(xn, wu_ref[...], preferred_element_type=F32)
    o_ref[...] = (a * jax.nn.sigmoid(a) * b).astype(o_ref.dtype)


def _ffn_up(x, g, wg, wu, layer, *, bm_pref=1024, bn_pref=512):
    M, K = x.shape
    N = wg.shape[2]
    bm = _pick(M, bm_pref)
    bn = _pick(N, bn_pref)
    pipelined = _nbytes((bm, K), F32) + 2 * _nbytes((K, bn), BF16) + _nbytes((bm, bn), BF16)
    resident = _nbytes((bm, K), BF16) + 4 * _nbytes((bm, bn), F32)
    return pl.pallas_call(
        _ffn_up_kernel,
        grid=(M // bm, N // bn),
        in_specs=[
            pl.BlockSpec((bm, K), lambda i, n: (i, 0)),
            pl.BlockSpec((1, K), lambda i, n: (0, 0)),
            _weight_spec(layer, K, bn),
            _weight_spec(layer, K, bn),
        ],
        out_specs=pl.BlockSpec((bm, bn), lambda i, n: (i, n)),
        out_shape=jax.ShapeDtypeStruct((M, N), BF16),
        scratch_shapes=[pltpu.VMEM((bm, K), BF16)],
        compiler_params=_compiler_params(2, pipelined, resident),
    )(x, g.reshape(1, K), wg, wu)


def _ple_kernel(x_ref, g_ref, wg_ref, p_ref, wp_ref, r_ref, o_ref, xn_ref):
    @pl.when(pl.program_id(1) == 0)
    def _():
        _rms_to_bf16(x_ref, g_ref, xn_ref)

    a = jnp.dot(xn_ref[...], wg_ref[...], preferred_element_type=F32)
    c = jnp.dot(p_ref[...].astype(BF16), wp_ref[...], preferred_element_type=F32)
    o_ref[...] = r_ref[...] + jax.nn.sigmoid(a) * c


def _ple(x, g, w_gate, p, w_proj, layer, *, bm_pref=1024, bn_pref=512):
    M, K = x.shape
    N = w_gate.shape[2]
    P = p.shape[1]
    bm = _pick(M, bm_pref)
    bn = _pick(N, bn_pref)
    pipelined = (_nbytes((bm, K), F32) + _nbytes((K, bn), BF16) + _nbytes((bm, P), F32)
                 + _nbytes((P, bn), BF16) + 2 * _nbytes((bm, bn), F32))
    resident = _nbytes((bm, K), BF16) + 3 * _nbytes((bm, bn), F32)
    return pl.pallas_call(
        _ple_kernel,
        grid=(M // bm, N // bn),
        in_specs=[
            pl.BlockSpec((bm, K), lambda i, n: (i, 0)),
            pl.BlockSpec((1, K), lambda i, n: (0, 0)),
            _weight_spec(layer, K, bn),
            pl.BlockSpec((bm, P), lambda i, n: (i, 0)),
            _weight_spec(layer, P, bn),
            pl.BlockSpec((bm, bn), lambda i, n: (i, n)),
        ],
        out_specs=pl.BlockSpec((bm, bn), lambda i, n: (i, n)),
        out_shape=jax.ShapeDtypeStruct((M, N), F32),
        scratch_shapes=[pltpu.VMEM((bm, K), BF16)],
        compiler_params=_compiler_params(2, pipelined, resident),
    )(x, g.reshape(1, K), w_gate, p, w_proj, x)


def _log2_gates(z2):
    neg_abs = pltpu.bitcast(pltpu.bitcast(z2, jnp.uint32) | jnp.uint32(0x80000000), F32)
    sp2 = jnp.log(1.0 + jnp.exp2(neg_abs)) * LOG2_E
    log_beta = jnp.minimum(z2, 0.0) - sp2
    return log_beta, log_beta - z2


def _suffix_sum(log_keep, tri):
    hi = log_keep.astype(BF16)
    lo = (log_keep - hi.astype(F32)).astype(BF16)
    return jnp.dot(hi, tri, preferred_element_type=F32) + jnp.dot(lo, tri, preferred_element_type=F32)


def _sb_attn_kernel(bias_ref, q_ref, k_ref, v_ref, o_ref, acc_ref, car_ref, *, tq, heads):
    hp = pl.program_id(1)
    i = pl.program_id(2)
    row = lax.broadcasted_iota(jnp.int32, (tq, tq), 0)
    col = lax.broadcasted_iota(jnp.int32, (tq, tq), 1)
    tri = jnp.where(row > col, 1.0, 0.0).astype(BF16)
    causal = col < row
    acc_ref[...] = jnp.zeros_like(acc_ref)
    car_ref[...] = jnp.zeros_like(car_ref)

    def visit(j, mask):
        koff = pl.multiple_of(j * tq, tq)
        lanes = [slice(g * HEAD_DIM, (g + 1) * HEAD_DIM) for g in range(heads)]
        logits = [lax.dot_general(q_ref[:, sl], k_ref[pl.ds(koff, tq), sl], NT_DIMS,
                                  preferred_element_type=F32) for sl in lanes]
        gates = []
        for g, z in enumerate(logits):
            log_beta, log_keep = _log2_gates(z * (ATTN_SCALE * LOG2_E) + bias_ref[hp * heads + g] * LOG2_E)
            if mask is not None:
                log_keep = jnp.where(mask, log_keep, 0.0)
            gates.append((log_beta, log_keep))
        later = [_suffix_sum(log_keep, tri) for _, log_keep in gates]
        weights = []
        for sl, (log_beta, log_keep), lt in zip(lanes, gates, later):
            carry = car_ref[:, sl]
            w = jnp.exp2(log_beta + lt + jnp.tile(carry, (1, tq // LANES)))
            if mask is not None:
                w = jnp.where(mask, w, 0.0)
            weights.append(w.astype(BF16))
            car_ref[:, sl] = carry + jnp.broadcast_to(jnp.sum(log_keep, axis=1, keepdims=True), carry.shape)
        for sl, w in zip(lanes, weights):
            acc_ref[:, sl] += jnp.dot(w, v_ref[pl.ds(koff, tq), sl], preferred_element_type=F32)

    visit(i, causal)

    def body(jj, c):
        visit(i - 1 - jj, None)
        return c

    lax.fori_loop(0, i, body, 0)
    o_ref[...] = acc_ref[...].astype(o_ref.dtype)


def _sb_attention(qkv, bias, B, S, *, tq_pref=256, heads_pref=8):
    _, M, W = qkv.shape
    tq = _pick(S, tq_pref)
    heads = _pick(W // HEAD_DIM, heads_pref)
    hw = heads * HEAD_DIM
    qkv4 = qkv.reshape(3, B, S, W)

    def kv_spec(slab):
        return pl.BlockSpec((None, None, S, hw), lambda b, h, i: (slab, b, 0, h), pipeline_mode=pl.Buffered(1))

    pipelined = 2 * _nbytes((tq, hw), BF16)
    resident = 2 * _nbytes((S, hw), BF16) + 2 * _nbytes((tq, hw), F32) + 6 * heads * _nbytes((tq, tq), F32)
    out = pl.pallas_call(
        functools.partial(_sb_attn_kernel, tq=tq, heads=heads),
        grid=(B, W // hw, S // tq),
        in_specs=[
            pl.BlockSpec(memory_space=pltpu.SMEM),
            pl.BlockSpec((None, None, tq, hw), lambda b, h, i: (0, b, i, h)),
            kv_spec(1),
            kv_spec(2),
        ],
        out_specs=pl.BlockSpec((None, tq, hw), lambda b, h, i: (b, i, h)),
        out_shape=jax.ShapeDtypeStruct((B, S, W), BF16),
        scratch_shapes=[pltpu.VMEM((tq, hw), F32), pltpu.VMEM((tq, hw), F32)],
        compiler_params=_compiler_params(3, pipelined, resident),
    )(bias, qkv4, qkv4, qkv4)
    return out.reshape(M, W)


def _sb_decode_kernel(pt_ref, q_ref, bias_ref, *refs, pages, heads):
    del pt_ref
    k_refs, v_refs = refs[:pages], refs[pages:2 * pages]
    o_ref, acc_ref, car_ref = refs[2 * pages:]
    p = pl.program_id(1)
    H = heads
    rows = k_refs[0].shape[0]
    chunks = rows // LANES
    SR = chunks * H

    @pl.when(p == 0)
    def _():
        acc_ref[...] = jnp.zeros_like(acc_ref)
        car_ref[...] = jnp.zeros_like(car_ref)

    own_head = (lax.broadcasted_iota(jnp.int32, (SR, LANES), 1) % H
                == lax.broadcasted_iota(jnp.int32, (SR, LANES), 0) % H)
    src = lax.broadcasted_iota(jnp.int32, (LANES, LANES), 0)
    dst = lax.broadcasted_iota(jnp.int32, (LANES, LANES), 1)
    tri = jnp.where((src // H > dst // H) & (src % H == dst % H), 1.0, 0.0).astype(BF16)
    q = q_ref[...].astype(BF16)

    scores = [lax.dot_general(q, k_ref[...].astype(BF16), NT_DIMS, preferred_element_type=F32)
              for k_ref in k_refs]
    gates = []
    for sc in scores:
        z = jnp.concatenate([sc[:, c * LANES:(c + 1) * LANES] for c in range(chunks)], axis=0)
        log_beta, log_keep = _log2_gates(z * (ATTN_SCALE * LOG2_E) + bias_ref[...])
        gates.append((log_beta, jnp.where(own_head, log_keep, 0.0)))
    within = [_suffix_sum(log_keep, tri) for _, log_keep in gates]
    run = car_ref[:, :1]
    weights = []
    for (log_beta, log_keep), inside in zip(gates, within):
        chunk_keep = jnp.sum(log_keep, axis=1, keepdims=True)
        after = [None] * chunks
        for c in reversed(range(chunks)):
            after[c] = run
            run = run + chunk_keep[c * H:(c + 1) * H]
        w = jnp.where(own_head, jnp.exp2(log_beta + inside + jnp.concatenate(after, axis=0)), 0.0)
        weights.append(jnp.concatenate([w[c * H:(c + 1) * H] for c in range(chunks)], axis=1).astype(BF16))
    car_ref[...] = jnp.broadcast_to(run, car_ref.shape)
    acc = acc_ref[...]
    for wm, v_ref in zip(weights, v_refs):
        acc = acc + jnp.dot(wm, v_ref[...].astype(BF16), preferred_element_type=F32)
    acc_ref[...] = acc

    @pl.when(p == pl.num_programs(1) - 1)
    def _():
        o_ref[...] = acc_ref[...]


def _sb_decode(q, cache_k, cache_v, layer, page_table, bias, *, pages_pref=4):
    DB, H, _ = q.shape
    n_layers, n_pool, page = cache_k.shape[:3]
    rows = page * H
    chunks = rows // LANES
    n_pages = page_table.shape[1]
    pages = _pick(n_pages, pages_pref)
    flat_k = cache_k.reshape(n_layers, n_pool, rows, HEAD_DIM)
    flat_v = cache_v.reshape(n_layers, n_pool, rows, HEAD_DIM)

    def page_spec(s):
        def index(b, p, pt):
            return (layer, pt[b, n_pages - 1 - (p * pages + s)], 0, 0)
        return pl.BlockSpec((None, None, rows, HEAD_DIM), index)

    pipelined = 2 * pages * _nbytes((rows, HEAD_DIM), F32)
    resident = pages * (2 * _nbytes((rows, HEAD_DIM), BF16) + 12 * _nbytes((chunks * H, LANES), F32))
    grid_spec = pltpu.PrefetchScalarGridSpec(
        num_scalar_prefetch=1,
        grid=(DB, n_pages // pages),
        in_specs=[
            pl.BlockSpec((None, H, HEAD_DIM), lambda b, p, pt: (b, 0, 0)),
            pl.BlockSpec((chunks * H, LANES), lambda b, p, pt: (0, 0)),
        ] + [page_spec(s) for s in range(pages)] * 2,
        out_specs=pl.BlockSpec((None, H, HEAD_DIM), lambda b, p, pt: (b, 0, 0)),
        scratch_shapes=[pltpu.VMEM((H, HEAD_DIM), F32), pltpu.VMEM((H, LANES), F32)],
    )
    return pl.pallas_call(
        functools.partial(_sb_decode_kernel, pages=pages, heads=H),
        grid_spec=grid_spec,
        out_shape=jax.ShapeDtypeStruct((DB, H, HEAD_DIM), F32),
        compiler_params=_compiler_params(2, pipelined, resident),
    )(page_table, q, jnp.tile(bias[:, None] * LOG2_E, (chunks, LANES)), *([flat_k] * pages), *([flat_v] * pages))


def _t5_bucket(dist):
    dist = np.asarray(dist, np.int64)
    max_exact = NUM_BUCKETS // 2
    large = max_exact + (np.log(np.maximum(dist, 1) / max_exact) / np.log(BUCKET_MAX_DIST / max_exact)
                         * (NUM_BUCKETS - max_exact)).astype(np.int64)
    large = np.minimum(large, NUM_BUCKETS - 1)
    return np.where(dist < max_exact, dist, large).astype(np.int32)


def _dil_offset_bias(rel_bias, g, heads):
    buckets = _t5_bucket(DIL_RATES[g] * np.arange(DIL_STEPS + 1))
    return rel_bias[g * heads:(g + 1) * heads][:, buckets].astype(F32)


def _dil_prompt_bias(offset_bias):
    heads = offset_bias.shape[0]
    S = DIL_STEPS
    by_shift = jnp.concatenate([jnp.full((heads, S - 1), MASK_VALUE, F32), offset_bias[:, ::-1],
                                jnp.full((heads, S), MASK_VALUE, F32)], axis=1)
    skew = jnp.tile(by_shift, (1, S))[:, :S * (3 * S - 1)].reshape(heads, S, 3 * S - 1)
    general = skew[:, :, S - 1:3 * S - 1]
    first = jnp.where(np.arange(2 * S)[None, None, :] >= S, general, MASK_VALUE)
    return jnp.stack([first, general])


def _dil_attn_kernel(q_ref, kc_ref, kp_ref, vc_ref, vp_ref, bias_ref, o_ref, l_ref, *, rate, unroll):
    not_first = jnp.minimum(pl.program_id(1), 1)
    heads = q_ref.shape[0]
    head0 = pl.program_id(2) * heads

    def body(t, c):
        tiles = []
        for u in range(unroll):
            r = t * unroll + u
            rows = pl.ds(r, DIL_STEPS, stride=rate) if rate > 1 else pl.ds(0, DIL_STEPS)
            tiles += [(h, rows) for h in range(heads)]
        scores = []
        for h, rows in tiles:
            keys = jnp.concatenate([kp_ref[h, rows, :], kc_ref[h, rows, :]], axis=0).astype(BF16)
            scores.append(lax.dot_general(q_ref[h, rows, :].astype(BF16), keys, NT_DIMS,
                                          preferred_element_type=F32))
        probs = []
        for (h, rows), s in zip(tiles, scores):
            s = s * ATTN_SCALE + bias_ref[not_first, head0 + h]
            m = jnp.max(s, axis=1, keepdims=True)
            p = jnp.exp(s - m)
            l = jnp.sum(p, axis=1, keepdims=True)
            l_ref[h, rows, :] = jnp.broadcast_to(m + jnp.log(l), (DIL_STEPS, HEAD_DIM))
            probs.append((p.astype(BF16), l))
        for (h, rows), (p, l) in zip(tiles, probs):
            vals = jnp.concatenate([vp_ref[h, rows, :], vc_ref[h, rows, :]], axis=0).astype(BF16)
            o_ref[h, rows, :] = jnp.dot(p, vals, preferred_element_type=F32) / l
        return c

    lax.fori_loop(0, rate // unroll, body, 0)


def _dil_group_attention(qkv, g, bias, B, S, *, block_bytes=2 * 2**20):
    n_slabs, M, _ = qkv.shape
    H = n_slabs // (3 * N_DIL_GROUPS)
    d = DIL_RATES[g]
    chunk = DIL_STEPS * d
    assert S % chunk == 0, "sequence must be a whole number of dilation chunks"
    nc = S // chunk
    hb = _pick(H, max(block_bytes // _nbytes((chunk, HEAD_DIM), F32), 1))
    blk = (hb, chunk, HEAD_DIM)

    def spec(slab, prev):
        base = slab * (H // hb)
        if prev:
            return pl.BlockSpec(blk, lambda b, c, h: (base + h, b * nc + jnp.maximum(c - 1, 0), 0))
        return pl.BlockSpec(blk, lambda b, c, h: (base + h, b * nc + c, 0))

    out_spec = pl.BlockSpec(blk, lambda b, c, h: (h, b * nc + c, 0))
    out_sds = jax.ShapeDtypeStruct((H, M, HEAD_DIM), F32)
    unroll = max(1, min(d, 8 // hb))
    pipelined = 7 * _nbytes(blk, F32)
    resident = 2 * _nbytes(bias.shape, F32) + 16 * unroll * hb * _nbytes((DIL_STEPS, 2 * DIL_STEPS), F32)
    return pl.pallas_call(
        functools.partial(_dil_attn_kernel, rate=d, unroll=unroll),
        grid=(B, nc, H // hb),
        in_specs=[
            spec(3 * g, False), spec(3 * g + 1, False), spec(3 * g + 1, True),
            spec(3 * g + 2, False), spec(3 * g + 2, True),
            pl.BlockSpec(bias.shape, lambda b, c, h: (0, 0, 0, 0)),
        ],
        out_specs=[out_spec, out_spec],
        out_shape=[out_sds, out_sds],
        compiler_params=_compiler_params(3, pipelined, resident),
    )(qkv, qkv, qkv, qkv, qkv, bias)


def _merge_groups(outs, lses):
    m = functools.reduce(jnp.maximum, lses)
    es = [jnp.exp(l - m) for l in lses]
    den = functools.reduce(jnp.add, es)
    num = functools.reduce(jnp.add, [e * o for e, o in zip(es, outs)])
    return num / den


def _dil_merge_outproj_kernel(*refs):
    G = N_DIL_GROUPS
    o_refs, l_refs = refs[:G], refs[G:2 * G]
    w_ref, r_ref, out_ref, mg_ref = refs[2 * G:]

    @pl.when(pl.program_id(1) == 0)
    def _():
        for h in range(o_refs[0].shape[0]):
            merged = _merge_groups([o[h] for o in o_refs], [l[h] for l in l_refs])
            mg_ref[:, h * HEAD_DIM:(h + 1) * HEAD_DIM] = merged.astype(BF16)

    out_ref[...] = r_ref[...] + jnp.dot(mg_ref[...], w_ref[...], preferred_element_type=F32)


def _dil_merge_outproj(outs, lses, w, layer, res, *, bm_pref=512, bn_pref=1024):
    H, M, _ = outs[0].shape
    K = H * HEAD_DIM
    N = w.shape[2]
    bm = _pick(M, bm_pref)
    bn = _pick(N, bn_pref)
    row_spec = pl.BlockSpec((H, bm, HEAD_DIM), lambda i, n: (0, i, 0))
    pipelined = 2 * N_DIL_GROUPS * _nbytes((bm, K), F32) + _nbytes((K, bn), BF16) + 2 * _nbytes((bm, bn), F32)
    resident = _nbytes((bm, K), BF16) + 6 * _nbytes((bm, K), F32)
    return pl.pallas_call(
        _dil_merge_outproj_kernel,
        grid=(M // bm, N // bn),
        in_specs=[row_spec] * (2 * N_DIL_GROUPS) + [
            _weight_spec(layer, K, bn),
            pl.BlockSpec((bm, bn), lambda i, n: (i, n)),
        ],
        out_specs=pl.BlockSpec((bm, bn), lambda i, n: (i, n)),
        out_shape=jax.ShapeDtypeStruct((M, N), F32),
        scratch_shapes=[pltpu.VMEM((bm, K), BF16)],
        compiler_params=_compiler_params(2, pipelined, resident),
    )(*outs, *lses, w, res)


def _dil_decode_kernel(qkv_ref, w0_ref, w1_ref, w2_ref, wb_ref, sb_ref, o_ref):
    Hd = o_ref.shape[0]
    outs, lses = [], []
    for g, win_ref in enumerate((w0_ref, w1_ref, w2_ref)):
        flat = win_ref[...].reshape(DIL_STEPS * 2 * Hd, HEAD_DIM).astype(BF16)
        q = qkv_ref[3 * g]
        k_new = qkv_ref[3 * g + 1]
        v_new = qkv_ref[3 * g + 2]
        s_win = lax.dot_general(q.astype(BF16), flat, NT_DIMS, preferred_element_type=F32) * ATTN_SCALE
        s_win = s_win + wb_ref[g]
        s_new = jnp.sum(q * k_new, axis=1, keepdims=True) * ATTN_SCALE + sb_ref[g][:, :1]
        m = jnp.maximum(jnp.max(s_win, axis=1, keepdims=True), s_new)
        p_win = jnp.exp(s_win - m)
        p_new = jnp.exp(s_new - m)
        l = jnp.sum(p_win, axis=1, keepdims=True) + p_new
        p_on_v = pltpu.roll(p_win, Hd, axis=1)
        pv = jnp.dot(p_on_v.astype(BF16), flat, preferred_element_type=F32) + p_new * v_new
        outs.append(pv / l)
        lses.append(m + jnp.log(l))
    o_ref[...] = _merge_groups(outs, lses)


def _dil_decode(qkv, states, layer, win_bias, self_bias):
    DB, n_slabs, Hd, _ = qkv.shape
    views = []
    for g, st in enumerate(states):
        L = st.shape[2]
        assert L == DIL_WINDOWS[g], "window state must hold exactly one window"
        d = DIL_RATES[g]
        views.append(st.reshape(st.shape[0], DB, L // d, d * 2 * Hd, HEAD_DIM))
    win_spec = pl.BlockSpec((None, None, DIL_STEPS, 2 * Hd, HEAD_DIM), lambda b: (layer, b, 0, 0, 0))
    pipelined = 3 * _nbytes((DIL_STEPS, 2 * Hd, HEAD_DIM), F32) + _nbytes((n_slabs, Hd, HEAD_DIM), F32)
    resident = (3 * _nbytes((DIL_STEPS, 2 * Hd, HEAD_DIM), F32) + 2 * _nbytes(win_bias.shape, F32)
                + 24 * _nbytes((Hd, DIL_STEPS * 2 * Hd), F32))
    return pl.pallas_call(
        _dil_decode_kernel,
        grid=(DB,),
        in_specs=[
            pl.BlockSpec((None, n_slabs, Hd, HEAD_DIM), lambda b: (b, 0, 0, 0)),
            win_spec, win_spec, win_spec,
            pl.BlockSpec(win_bias.shape, lambda b: (0, 0, 0)),
            pl.BlockSpec(self_bias.shape, lambda b: (0, 0, 0)),
        ],
        out_specs=pl.BlockSpec((None, Hd, HEAD_DIM), lambda b: (b, 0, 0)),
        out_shape=jax.ShapeDtypeStruct((DB, Hd, HEAD_DIM), F32),
        compiler_params=_compiler_params(1, pipelined, resident),
    )(qkv, *views, win_bias, self_bias)


def _dil_decode_bias(offset_bias):
    heads = offset_bias.shape[0]
    own_k = np.arange(2 * heads)[None, :] == np.arange(heads)[:, None]
    table = jnp.where(own_k[:, None, :], offset_bias[:, :0:-1, None], MASK_VALUE)
    return table.reshape(heads, -1), jnp.broadcast_to(offset_bias[:, :1], (heads, LANES))


def _pad_rows(x, rows):
    return jnp.pad(x, ((0, rows - x.shape[0]), (0, 0)))


def kernel(x_prompt, x_sample, cache_sb_k, cache_sb_v, state_win0_kv, state_win1_kv, state_win2_kv, page_table, p_prompt, p_sample, g_mix, g_ffn, g_ple, w_qkv_sb, w_o_sb, b_sb, w_qkv_dil, w_o_dil, g_qnorm_dil, g_knorm_dil, rel_bias, w_ffn_gate, w_ffn_up, w_ffn_down, w_ple_proj, w_ple_gate):
    B, S, D = x_prompt.shape
    DB, T, _ = x_sample.shape
    assert T == 1, "decode step handles one new token per sample row"
    depth = g_mix.shape[0]
    M = B * S
    sb_heads = D // HEAD_DIM
    dil_w = w_o_dil.shape[1]
    dil_heads = dil_w // HEAD_DIM
    win_states = (state_win0_kv, state_win1_kv, state_win2_kv)

    hp = x_prompt.reshape(M, D)
    hs = _pad_rows(x_sample.reshape(DB, D), SAMPLE_ROWS)

    w_qkv_sb, w_o_sb, w_qkv_dil, w_o_dil, w_ffn_gate, w_ffn_up, w_ffn_down, w_ple_proj, w_ple_gate = [
        _cast_bf16(w) for w in (w_qkv_sb, w_o_sb, w_qkv_dil, w_o_dil, w_ffn_gate, w_ffn_up, w_ffn_down,
                                w_ple_proj, w_ple_gate)]

    offset_bias = [_dil_offset_bias(rel_bias, g, dil_heads) for g in range(N_DIL_GROUPS)]
    prompt_bias = [_dil_prompt_bias(ob) for ob in offset_bias]
    win_bias, self_bias = (jnp.stack(t) for t in zip(*[_dil_decode_bias(ob) for ob in offset_bias]))

    sb_kp, sb_vp, sb_ks, sb_vs = [], [], [], []
    win_p = [[] for _ in range(N_DIL_GROUPS)]
    win_s = [[] for _ in range(N_DIL_GROUPS)]

    for i in range(depth):
        j = i // 2
        if i % 2 == 0:
            kv, qkv = _sb_qkv(hp, g_mix[i], w_qkv_sb, j)
            attn = _sb_attention(qkv, b_sb[j], B, S)
            hp = _matmul_res(attn, w_o_sb, j, hp)
            sb_kp.append(kv[0].reshape(B, S, sb_heads, HEAD_DIM))
            sb_vp.append(kv[1].reshape(B, S, sb_heads, HEAD_DIM))

            kv_s, qkv_s = _sb_qkv(hs, g_mix[i], w_qkv_sb, j)
            q_s = qkv_s[0, :DB].astype(F32).reshape(DB, sb_heads, HEAD_DIM)
            attn_s = _sb_decode(q_s, cache_sb_k, cache_sb_v, j, page_table, b_sb[j])
            hs = _matmul_res(_pad_rows(attn_s.reshape(DB, D), SAMPLE_ROWS), w_o_sb, j, hs)
            sb_ks.append(kv_s[0, :DB].reshape(DB, 1, sb_heads, HEAD_DIM))
            sb_vs.append(kv_s[1, :DB].reshape(DB, 1, sb_heads, HEAD_DIM))
        else:
            ones = jnp.ones((HEAD_DIM,), F32)
            head_gain = jnp.stack([jnp.tile(gain, dil_heads)
                                   for g in range(N_DIL_GROUPS)
                                   for gain in (g_qnorm_dil[j, g], g_knorm_dil[j, g], ones)])[:, None, :]
            qkv = _dil_qkv(hp, g_mix[i], w_qkv_dil, j, head_gain)
            outs, lses = zip(*[_dil_group_attention(qkv, g, prompt_bias[g], B, S)
                               for g in range(N_DIL_GROUPS)])
            hp = _dil_merge_outproj(outs, lses, w_o_dil, j, hp)
            kv6 = qkv.reshape(N_DIL_GROUPS, 3, dil_heads, B, S, HEAD_DIM)
            for g in range(N_DIL_GROUPS):
                L = min(DIL_WINDOWS[g], S)
                win_p[g].append(jnp.transpose(kv6[g, 1:3, :, :, S - L:], (2, 3, 0, 1, 4)))

            qkv_s = _dil_qkv(hs, g_mix[i], w_qkv_dil, j, head_gain)[:, :DB]
            qkv_s = jnp.transpose(qkv_s.reshape(3 * N_DIL_GROUPS, dil_heads, DB, HEAD_DIM), (2, 0, 1, 3))
            attn_s = _dil_decode(qkv_s, win_states, j, win_bias, self_bias)
            hs = _matmul_res(_pad_rows(attn_s.reshape(DB, dil_w), SAMPLE_ROWS), w_o_dil, j, hs)
            for g in range(N_DIL_GROUPS):
                new_row = qkv_s[:, None, 3 * g + 1:3 * g + 3]
                win_s[g].append(jnp.concatenate([win_states[g][j][:, 1:], new_row], axis=1))

        hp = _matmul_res(_ffn_up(hp, g_ffn[i], w_ffn_gate, w_ffn_up, i), w_ffn_down, i, hp)
        hp = _ple(hp, g_ple[i], w_ple_gate, p_prompt[i].reshape(M, -1), w_ple_proj, i)
        hs = _matmul_res(_ffn_up(hs, g_ffn[i], w_ffn_gate, w_ffn_up, i), w_ffn_down, i, hs)
        hs = _ple(hs, g_ple[i], w_ple_gate, _pad_rows(p_sample[i].reshape(DB, -1), SAMPLE_ROWS), w_ple_proj, i)

    return (hp.reshape(B, S, D), hs[:DB].reshape(DB, 1, D),
            jnp.stack(sb_kp), jnp.stack(sb_vp), jnp.stack(sb_ks), jnp.stack(sb_vs),
            jnp.stack(win_p[0]), jnp.stack(win_p[1]), jnp.stack(win_p[2]),
            jnp.stack(win_s[0]), jnp.stack(win_s[1]), jnp.stack(win_s[2]))
```

```python
import functools

import numpy as np
import jax
import jax.numpy as jnp
from jax import lax
from jax.experimental import pallas as pl
from jax.experimental.pallas import tpu as pltpu

F32 = jnp.float32
BF16 = jnp.bfloat16

HEAD_DIM = 128
LANES = 128
RMS_EPS = 1e-6
ATTN_SCALE = HEAD_DIM ** -0.5
LOG2_E = float(np.log2(np.e))
DIL_WINDOWS = (128, 512, 2048)
DIL_RATES = (1, 4, 16)
DIL_STEPS = 128
N_DIL_GROUPS = 3
NUM_BUCKETS = 32
BUCKET_MAX_DIST = 2048
MASK_VALUE = -1e30
SAMPLE_ROWS = 16

V7X_VMEM_BYTES = 64 * 2**20
VMEM_CEILING = V7X_VMEM_BYTES - 8 * 2**20

NT_DIMS = (((1,), (1,)), ((), ()))


def _compiler_params(n_grid, pipelined_bytes, resident_bytes):
    need = 2 * pipelined_bytes + resident_bytes + 4 * 2**20
    return pltpu.CompilerParams(
        dimension_semantics=("arbitrary",) * n_grid,
        vmem_limit_bytes=int(min(max(need, 16 * 2**20), VMEM_CEILING)))


def _nbytes(shape, dtype):
    return int(np.prod(shape)) * jnp.dtype(dtype).itemsize


def _pick(total, preferred):
    b = min(total, preferred)
    while total % b:
        b //= 2
    return b


def _cast_kernel(w_ref, o_ref):
    o_ref[...] = w_ref[...].astype(o_ref.dtype)


def _cast_bf16(w, *, bn_pref=2048, block_bytes=4 * 2**20):
    L, K, N = w.shape
    bn = _pick(N, bn_pref)
    bk = _pick(K, block_bytes // (4 * bn))
    spec = pl.BlockSpec((None, bk, bn), lambda l, k, n: (l, k, n))
    return pl.pallas_call(
        _cast_kernel,
        grid=(L, K // bk, N // bn),
        in_specs=[spec],
        out_specs=spec,
        out_shape=jax.ShapeDtypeStruct(w.shape, BF16),
        compiler_params=_compiler_params(3, _nbytes((bk, bn), F32) + _nbytes((bk, bn), BF16), 0),
    )(w)


def _weight_spec(layer, K, bn):
    return pl.BlockSpec((None, K, bn), lambda i, n: (layer, 0, n))


def _rms_to_bf16(x_ref, g_ref, xn_ref):
    x = x_ref[...]
    ms = jnp.mean(x * x, axis=-1, keepdims=True)
    xn_ref[...] = (x * lax.rsqrt(ms + RMS_EPS) * g_ref[...]).astype(BF16)


def _qkv_kernel(x_ref, g_ref, w_ref, k_in_ref, v_in_ref, k_ref, v_ref, qkv_ref, xn_ref, *, per):
    del k_in_ref, v_in_ref
    n = pl.program_id(1)

    @pl.when(n == 0)
    def _():
        _rms_to_bf16(x_ref, g_ref, xn_ref)

    y = jnp.dot(xn_ref[...], w_ref[...], preferred_element_type=F32)
    qkv_ref[...] = y.astype(BF16)

    def by_head(ref):
        for h in range(ref.shape[1]):
            ref[:, h, :] = y[:, h * HEAD_DIM:(h + 1) * HEAD_DIM]

    @pl.when(jnp.logical_and(n >= per, n < 2 * per))
    def _():
        by_head(k_ref)

    @pl.when(n >= 2 * per)
    def _():
        by_head(v_ref)


def _sb_qkv(x, g, w, layer, kv, *, bm_pref=1024, bn_pref=1024):
    M, K = x.shape
    L, _, N = w.shape
    D = N // 3
    H = D // HEAD_DIM
    bm = _pick(M, bm_pref)
    bn = _pick(D, bn_pref)
    hb = bn // HEAD_DIM
    per = D // bn

    def kv_spec(slab):
        return pl.BlockSpec((None, bm, hb, HEAD_DIM),
                            lambda i, n: (layer, i, jnp.clip(n - slab * per, 0, per - 1), 0))

    kv_sds = jax.ShapeDtypeStruct((L, M, H, HEAD_DIM), F32)
    pipelined = _nbytes((bm, K), F32) + _nbytes((K, bn), BF16) + 2 * _nbytes((bm, bn), F32) + _nbytes((bm, bn), BF16)
    resident = _nbytes((bm, K), BF16) + 2 * _nbytes((bm, bn), F32)
    return pl.pallas_call(
        functools.partial(_qkv_kernel, per=per),
        grid=(M // bm, N // bn),
        in_specs=[
            pl.BlockSpec((bm, K), lambda i, n: (i, 0)),
            pl.BlockSpec((1, K), lambda i, n: (0, 0)),
            _weight_spec(layer, K, bn),
        ] + [pl.BlockSpec(memory_space=pl.ANY)] * 2,
        out_specs=[kv_spec(1), kv_spec(2), pl.BlockSpec((None, bm, bn), lambda i, n: (n // per, i, n % per))],
        out_shape=[kv_sds, kv_sds, jax.ShapeDtypeStruct((3, M, D), BF16)],
        input_output_aliases={3: 0, 4: 1},
        scratch_shapes=[pltpu.VMEM((bm, K), BF16)],
        compiler_params=_compiler_params(2, pipelined, resident),
    )(x, g.reshape(1, K), w, *kv)


def _dil_qkv_kernel(x_ref, g_ref, w_ref, hg_ref, o_ref, xn_ref):
    n = pl.program_id(1)

    @pl.when(n == 0)
    def _():
        _rms_to_bf16(x_ref, g_ref, xn_ref)

    y = jnp.dot(xn_ref[...], w_ref[...], preferred_element_type=F32)
    is_v = (n % 3) == 2
    heads = o_ref.shape[0]

    @pl.when(is_v)
    def _():
        for h in range(heads):
            o_ref[h] = y[:, h * HEAD_DIM:(h + 1) * HEAD_DIM]

    @pl.when(jnp.logical_not(is_v))
    def _():
        for h in range(heads):
            sl = slice(h * HEAD_DIM, (h + 1) * HEAD_DIM)
            yh = y[:, sl]
            ms = jnp.mean(yh * yh, axis=-1, keepdims=True)
            o_ref[h] = yh * lax.rsqrt(ms + RMS_EPS) * hg_ref[:, sl]


def _dil_qkv(x, g, w, layer, head_gain, *, bm_pref=512):
    M, K = x.shape
    N = w.shape[2]
    W = head_gain.shape[2]
    heads = W // HEAD_DIM
    bm = _pick(M, bm_pref)
    pipelined = _nbytes((bm, K), F32) + _nbytes((K, W), BF16) + _nbytes((bm, W), F32)
    resident = _nbytes((bm, K), BF16) + 2 * _nbytes((bm, W), F32)
    return pl.pallas_call(
        _dil_qkv_kernel,
        grid=(M // bm, N // W),
        in_specs=[
            pl.BlockSpec((bm, K), lambda i, n: (i, 0)),
            pl.BlockSpec((1, K), lambda i, n: (0, 0)),
            _weight_spec(layer, K, W),
            pl.BlockSpec((None, 1, W), lambda i, n: (n, 0, 0)),
        ],
        out_specs=pl.BlockSpec((heads, bm, HEAD_DIM), lambda i, n: (n, i, 0)),
        out_shape=jax.ShapeDtypeStruct((N // HEAD_DIM, M, HEAD_DIM), F32),
        scratch_shapes=[pltpu.VMEM((bm, K), BF16)],
        compiler_params=_compiler_params(2, pipelined, resident),
    )(x, g.reshape(1, K), w, head_gain)


def _matmul_res_kernel(x_ref, w_ref, r_ref, o_ref):
    o_ref[...] = r_ref[...] + jnp.dot(x_ref[...].astype(BF16), w_ref[...], preferred_element_type=F32)


def _matmul_res(x, w, layer, res, *, bm_pref=512, bn_pref=512):
    M, K = x.shape
    N = w.shape[2]
    bm = _pick(M, bm_pref)
    bn = _pick(N, bn_pref)
    pipelined = _nbytes((bm, K), x.dtype) + _nbytes((K, bn), BF16) + 2 * _nbytes((bm, bn), F32)
    resident = (0 if x.dtype == BF16 else _nbytes((bm, K), BF16)) + _nbytes((bm, bn), F32)
    return pl.pallas_call(
        _matmul_res_kernel,
        grid=(M // bm, N // bn),
        in_specs=[
            pl.BlockSpec((bm, K), lambda i, n: (i, 0)),
            _weight_spec(layer, K, bn),
            pl.BlockSpec((bm, bn), lambda i, n: (i, n)),
        ],
        out_specs=pl.BlockSpec((bm, bn), lambda i, n: (i, n)),
        out_shape=jax.ShapeDtypeStruct((M, N), F32),
        compiler_params=_compiler_params(2, pipelined, resident),
    )(x, w, res)


def _ffn_up_kernel(x_ref, g_ref, wg_ref, wu_ref, o_ref, xn_ref):
    @pl.when(pl.program_id(1) == 0)
    def _():
        _rms_to_bf16(x_ref, g_ref, xn_ref)

    xn = xn_ref[...]
    a = jnp.dot(xn, wg_ref[...], preferred_element_type=F32)
    b = jnp.dot(xn, wu_ref[...], preferred_element_type=F32)
    o_ref[...] = (a * jax.nn.sigmoid(a) * b).astype(o_ref.dtype)


def _ffn_up(x, g, wg, wu, layer, *, bm_pref=1024, bn_pref=512):
    M, K = x.shape
    N = wg.shape[2]
    bm = _pick(M, bm_pref)
    bn = _pick(N, bn_pref)
    pipelined = _nbytes((bm, K), F32) + 2 * _nbytes((K, bn), BF16) + _nbytes((bm, bn), BF16)
    resident = _nbytes((bm, K), BF16) + 4 * _nbytes((bm, bn), F32)
    return pl.pallas_call(
        _ffn_up_kernel,
        grid=(M // bm, N // bn),
        in_specs=[
            pl.BlockSpec((bm, K), lambda i, n: (i, 0)),
            pl.BlockSpec((1, K), lambda i, n: (0, 0)),
            _weight_spec(layer, K, bn),
            _weight_spec(layer, K, bn),
        ],
        out_specs=pl.BlockSpec((bm, bn), lambda i, n: (i, n)),
        out_shape=jax.ShapeDtypeStruct((M, N), BF16),
        scratch_shapes=[pltpu.VMEM((bm, K), BF16)],
        compiler_params=_compiler_params(2, pipelined, resident),
    )(x, g.reshape(1, K), wg, wu)


def _ple_kernel(x_ref, g_ref, wg_ref, p_ref, wp_ref, o_ref, xn_ref):
    n = pl.program_id(1)

    @pl.when(n == 0)
    def _():
        _rms_to_bf16(x_ref, g_ref, xn_ref)

    bn = o_ref.shape[1]
    a = jnp.dot(xn_ref[...], wg_ref[...], preferred_element_type=F32)
    c = jnp.dot(p_ref[...].astype(BF16), wp_ref[...], preferred_element_type=F32)
    o_ref[...] = x_ref[:, pl.ds(pl.multiple_of(n * bn, bn), bn)] + jax.nn.sigmoid(a) * c


def _ple(x, g, w_gate, p, w_proj, layer, *, bm_pref=1024, bn_pref=512):
    M, K = x.shape
    N = w_gate.shape[2]
    P = p.shape[1]
    bm = _pick(M, bm_pref)
    bn = _pick(N, bn_pref)
    pipelined = (_nbytes((bm, K), F32) + _nbytes((K, bn), BF16) + _nbytes((bm, P), F32)
                 + _nbytes((P, bn), BF16) + _nbytes((bm, bn), F32))
    resident = _nbytes((bm, K), BF16) + 3 * _nbytes((bm, bn), F32)
    return pl.pallas_call(
        _ple_kernel,
        grid=(M // bm, N // bn),
        in_specs=[
            pl.BlockSpec((bm, K), lambda i, n: (i, 0)),
            pl.BlockSpec((1, K), lambda i, n: (0, 0)),
            _weight_spec(layer, K, bn),
            pl.BlockSpec((bm, P), lambda i, n: (i, 0)),
            _weight_spec(layer, P, bn),
        ],
        out_specs=pl.BlockSpec((bm, bn), lambda i, n: (i, n)),
        out_shape=jax.ShapeDtypeStruct((M, N), F32),
        scratch_shapes=[pltpu.VMEM((bm, K), BF16)],
        compiler_params=_compiler_params(2, pipelined, resident),
    )(x, g.reshape(1, K), w_gate, p, w_proj)


def _log2_gates(z2):
    neg_abs = pltpu.bitcast(pltpu.bitcast(z2, jnp.uint32) | jnp.uint32(0x80000000), F32)
    sp2 = jnp.log(1.0 + jnp.exp2(neg_abs)) * LOG2_E
    log_beta = jnp.minimum(z2, 0.0) - sp2
    return log_beta, log_beta - z2


def _suffix_sum(log_keep, tri):
    return jnp.dot(log_keep.astype(BF16), tri, preferred_element_type=F32)


def _sb_attn_kernel(bias_ref, q_ref, k_ref, v_ref, o_ref, acc_ref, car_ref, *, tq, heads):
    hp = pl.program_id(1)
    i = pl.program_id(2)
    row = lax.broadcasted_iota(jnp.int32, (tq, tq), 0)
    col = lax.broadcasted_iota(jnp.int32, (tq, tq), 1)
    tri = jnp.where(row > col, 1.0, 0.0).astype(BF16)
    causal = col < row
    acc_ref[...] = jnp.zeros_like(acc_ref)
    car_ref[...] = jnp.zeros_like(car_ref)

    def visit(j, mask):
        koff = pl.multiple_of(j * tq, tq)
        lanes = [slice(g * HEAD_DIM, (g + 1) * HEAD_DIM) for g in range(heads)]
        logits = [lax.dot_general(q_ref[:, sl], k_ref[pl.ds(koff, tq), sl], NT_DIMS,
                                  preferred_element_type=F32) for sl in lanes]
        gates = []
        for g, z in enumerate(logits):
            log_beta, log_keep = _log2_gates(z * (ATTN_SCALE * LOG2_E) + bias_ref[hp * heads + g] * LOG2_E)
            if mask is not None:
                log_keep = jnp.where(mask, log_keep, 0.0)
            gates.append((log_beta, log_keep))
        later = [_suffix_sum(log_keep, tri) for _, log_keep in gates]
        weights = []
        for sl, (log_beta, log_keep), lt in zip(lanes, gates, later):
            carry = car_ref[:, sl]
            w = jnp.exp2(log_beta + lt + jnp.tile(carry, (1, tq // LANES)))
            if mask is not None:
                w = jnp.where(mask, w, 0.0)
            weights.append(w.astype(BF16))
            car_ref[:, sl] = carry + jnp.broadcast_to(jnp.sum(log_keep, axis=1, keepdims=True), carry.shape)
        for sl, w in zip(lanes, weights):
            acc_ref[:, sl] += jnp.dot(w, v_ref[pl.ds(koff, tq), sl], preferred_element_type=F32)

    visit(i, causal)

    def body(jj, c):
        visit(i - 1 - jj, None)
        return c

    lax.fori_loop(0, i, body, 0)
    o_ref[...] = acc_ref[...].astype(o_ref.dtype)


def _sb_attention(qkv, bias, B, S, *, tq_pref=256, heads_pref=8):
    _, M, W = qkv.shape
    tq = _pick(S, tq_pref)
    heads = _pick(W // HEAD_DIM, heads_pref)
    hw = heads * HEAD_DIM
    qkv4 = qkv.reshape(3, B, S, W)

    def kv_spec(slab):
        return pl.BlockSpec((None, None, S, hw), lambda b, h, i: (slab, b, 0, h), pipeline_mode=pl.Buffered(1))

    pipelined = 2 * _nbytes((tq, hw), BF16)
    resident = 2 * _nbytes((S, hw), BF16) + 2 * _nbytes((tq, hw), F32) + 6 * heads * _nbytes((tq, tq), F32)
    out = pl.pallas_call(
        functools.partial(_sb_attn_kernel, tq=tq, heads=heads),
        grid=(B, W // hw, S // tq),
        in_specs=[
            pl.BlockSpec(memory_space=pltpu.SMEM),
            pl.BlockSpec((None, None, tq, hw), lambda b, h, i: (0, b, i, h)),
            kv_spec(1),
            kv_spec(2),
        ],
        out_specs=pl.BlockSpec((None, tq, hw), lambda b, h, i: (b, i, h)),
        out_shape=jax.ShapeDtypeStruct((B, S, W), BF16),
        scratch_shapes=[pltpu.VMEM((tq, hw), F32), pltpu.VMEM((tq, hw), F32)],
        compiler_params=_compiler_params(3, pipelined, resident),
    )(bias, qkv4, qkv4, qkv4)
    return out.reshape(M, W)


def _sb_decode_kernel(pt_ref, q_ref, bias_ref, *refs, pages, heads):
    del pt_ref
    k_refs, v_refs = refs[:pages], refs[pages:2 * pages]
    o_ref, acc_ref, car_ref = refs[2 * pages:]
    p = pl.program_id(1)
    H = heads
    rows = k_refs[0].shape[0]
    chunks = rows // LANES
    SR = chunks * H

    @pl.when(p == 0)
    def _():
        acc_ref[...] = jnp.zeros_like(acc_ref)
        car_ref[...] = jnp.zeros_like(car_ref)

    own_head = (lax.broadcasted_iota(jnp.int32, (SR, LANES), 1) % H
                == lax.broadcasted_iota(jnp.int32, (SR, LANES), 0) % H)
    src = lax.broadcasted_iota(jnp.int32, (LANES, LANES), 0)
    dst = lax.broadcasted_iota(jnp.int32, (LANES, LANES), 1)
    tri = jnp.where((src // H > dst // H) & (src % H == dst % H), 1.0, 0.0).astype(BF16)
    q = q_ref[...].astype(BF16)

    scores = [lax.dot_general(q, k_ref[...].astype(BF16), NT_DIMS, preferred_element_type=F32)
              for k_ref in k_refs]
    gates = []
    for sc in scores:
        z = jnp.concatenate([sc[:, c * LANES:(c + 1) * LANES] for c in range(chunks)], axis=0)
        log_beta, log_keep = _log2_gates(z * (ATTN_SCALE * LOG2_E) + bias_ref[...])
        gates.append((log_beta, jnp.where(own_head, log_keep, 0.0)))
    within = [_suffix_sum(log_keep, tri) for _, log_keep in gates]
    run = car_ref[:, :1]
    weights = []
    for (log_beta, log_keep), inside in zip(gates, within):
        chunk_keep = jnp.sum(log_keep, axis=1, keepdims=True)
        after = [None] * chunks
        for c in reversed(range(chunks)):
            after[c] = run
            run = run + chunk_keep[c * H:(c + 1) * H]
        w = jnp.where(own_head, jnp.exp2(log_beta + inside + jnp.concatenate(after, axis=0)), 0.0)
        weights.append(jnp.concatenate([w[c * H:(c + 1) * H] for c in range(chunks)], axis=1).astype(BF16))
    car_ref[...] = jnp.broadcast_to(run, car_ref.shape)
    acc = acc_ref[...]
    for wm, v_ref in zip(weights, v_refs):
        acc = acc + jnp.dot(wm, v_ref[...].astype(BF16), preferred_element_type=F32)
    acc_ref[...] = acc

    @pl.when(p == pl.num_programs(1) - 1)
    def _():
        o_ref[...] = acc_ref[...]


def _sb_decode(q, cache_k, cache_v, layer, page_table, bias, *, pages_pref=4):
    DB, H, _ = q.shape
    n_layers, n_pool, page = cache_k.shape[:3]
    rows = page * H
    chunks = rows // LANES
    n_pages = page_table.shape[1]
    pages = _pick(n_pages, pages_pref)
    flat_k = cache_k.reshape(n_layers, n_pool, rows, HEAD_DIM)
    flat_v = cache_v.reshape(n_layers, n_pool, rows, HEAD_DIM)

    def page_spec(s):
        def index(b, p, pt):
            return (layer, pt[b, n_pages - 1 - (p * pages + s)], 0, 0)
        return pl.BlockSpec((None, None, rows, HEAD_DIM), index)

    pipelined = 2 * pages * _nbytes((rows, HEAD_DIM), F32)
    resident = pages * (2 * _nbytes((rows, HEAD_DIM), BF16) + 12 * _nbytes((chunks * H, LANES), F32))
    grid_spec = pltpu.PrefetchScalarGridSpec(
        num_scalar_prefetch=1,
        grid=(DB, n_pages // pages),
        in_specs=[
            pl.BlockSpec((None, H, HEAD_DIM), lambda b, p, pt: (b, 0, 0)),
            pl.BlockSpec((chunks * H, LANES), lambda b, p, pt: (0, 0)),
        ] + [page_spec(s) for s in range(pages)] * 2,
        out_specs=pl.BlockSpec((None, H, HEAD_DIM), lambda b, p, pt: (b, 0, 0)),
        scratch_shapes=[pltpu.VMEM((H, HEAD_DIM), F32), pltpu.VMEM((H, LANES), F32)],
    )
    return pl.pallas_call(
        functools.partial(_sb_decode_kernel, pages=pages, heads=H),
        grid_spec=grid_spec,
        out_shape=jax.ShapeDtypeStruct((DB, H, HEAD_DIM), F32),
        compiler_params=_compiler_params(2, pipelined, resident),
    )(page_table, q, jnp.tile(bias[:, None] * LOG2_E, (chunks, LANES)), *([flat_k] * pages), *([flat_v] * pages))


def _t5_bucket(dist):
    dist = np.asarray(dist, np.int64)
    max_exact = NUM_BUCKETS // 2
    large = max_exact + (np.log(np.maximum(dist, 1) / max_exact) / np.log(BUCKET_MAX_DIST / max_exact)
                         * (NUM_BUCKETS - max_exact)).astype(np.int64)
    large = np.minimum(large, NUM_BUCKETS - 1)
    return np.where(dist < max_exact, dist, large).astype(np.int32)


def _dil_offset_bias(rel_bias, g, heads):
    buckets = _t5_bucket(DIL_RATES[g] * np.arange(DIL_STEPS + 1))
    return rel_bias[g * heads:(g + 1) * heads][:, buckets].astype(F32)


def _dil_prompt_bias(offset_bias):
    heads = offset_bias.shape[0]
    S = DIL_STEPS
    by_shift = jnp.concatenate([jnp.full((heads, S - 1), MASK_VALUE, F32), offset_bias[:, ::-1],
                                jnp.full((heads, S), MASK_VALUE, F32)], axis=1)
    skew = jnp.tile(by_shift, (1, S))[:, :S * (3 * S - 1)].reshape(heads, S, 3 * S - 1)
    general = skew[:, :, S - 1:3 * S - 1]
    first = jnp.where(np.arange(2 * S)[None, None, :] >= S, general, MASK_VALUE)
    return jnp.stack([first, general])


def _dil_attn_kernel(q_ref, kc_ref, kp_ref, vc_ref, vp_ref, bias_ref, o_ref, l_ref, *, rate, unroll):
    not_first = jnp.minimum(pl.program_id(1), 1)
    heads = q_ref.shape[0]
    head0 = pl.program_id(2) * heads

    def body(t, c):
        tiles = []
        for u in range(unroll):
            r = t * unroll + u
            rows = pl.ds(r, DIL_STEPS, stride=rate) if rate > 1 else pl.ds(0, DIL_STEPS)
            tiles += [(h, rows) for h in range(heads)]
        scores = []
        for h, rows in tiles:
            keys = jnp.concatenate([kp_ref[h, rows, :], kc_ref[h, rows, :]], axis=0).astype(BF16)
            scores.append(lax.dot_general(q_ref[h, rows, :].astype(BF16), keys, NT_DIMS,
                                          preferred_element_type=F32))
        probs = []
        for (h, rows), s in zip(tiles, scores):
            s = s * ATTN_SCALE + bias_ref[not_first, head0 + h]
            m = jnp.max(s, axis=1, keepdims=True)
            p = jnp.exp(s - m)
            l = jnp.sum(p, axis=1, keepdims=True)
            l_ref[h, rows, :] = jnp.broadcast_to(m + jnp.log(l), (DIL_STEPS, HEAD_DIM))
            probs.append((p.astype(BF16), l))
        for (h, rows), (p, l) in zip(tiles, probs):
            vals = jnp.concatenate([vp_ref[h, rows, :], vc_ref[h, rows, :]], axis=0).astype(BF16)
            o_ref[h, rows, :] = jnp.dot(p, vals, preferred_element_type=F32) / l
        return c

    lax.fori_loop(0, rate // unroll, body, 0)


def _dil_group_attention(qkv, g, bias, B, S, *, block_bytes=2 * 2**20):
    n_slabs, M, _ = qkv.shape
    H = n_slabs // (3 * N_DIL_GROUPS)
    d = DIL_RATES[g]
    chunk = DIL_STEPS * d
    assert S % chunk == 0, "sequence must be a whole number of dilation chunks"
    nc = S // chunk
    hb = _pick(H, max(block_bytes // _nbytes((chunk, HEAD_DIM), F32), 1))
    blk = (hb, chunk, HEAD_DIM)

    def spec(slab, prev):
        base = slab * (H // hb)
        if prev:
            return pl.BlockSpec(blk, lambda b, c, h: (base + h, b * nc + jnp.maximum(c - 1, 0), 0))
        return pl.BlockSpec(blk, lambda b, c, h: (base + h, b * nc + c, 0))

    out_spec = pl.BlockSpec(blk, lambda b, c, h: (h, b * nc + c, 0))
    out_sds = jax.ShapeDtypeStruct((H, M, HEAD_DIM), F32)
    unroll = max(1, min(d, 8 // hb))
    pipelined = 7 * _nbytes(blk, F32)
    resident = 2 * _nbytes(bias.shape, F32) + 16 * unroll * hb * _nbytes((DIL_STEPS, 2 * DIL_STEPS), F32)
    return pl.pallas_call(
        functools.partial(_dil_attn_kernel, rate=d, unroll=unroll),
        grid=(B, nc, H // hb),
        in_specs=[
            spec(3 * g, False), spec(3 * g + 1, False), spec(3 * g + 1, True),
            spec(3 * g + 2, False), spec(3 * g + 2, True),
            pl.BlockSpec(bias.shape, lambda b, c, h: (0, 0, 0, 0)),
        ],
        out_specs=[out_spec, out_spec],
        out_shape=[out_sds, out_sds],
        compiler_params=_compiler_params(3, pipelined, resident),
    )(qkv, qkv, qkv, qkv, qkv, bias)


def _merge_groups(outs, lses):
    m = functools.reduce(jnp.maximum, lses)
    es = [jnp.exp(l - m) for l in lses]
    den = functools.reduce(jnp.add, es)
    num = functools.reduce(jnp.add, [e * o for e, o in zip(es, outs)])
    return num / den


def _dil_merge_outproj_kernel(*refs):
    G = N_DIL_GROUPS
    o_refs, l_refs = refs[:G], refs[G:2 * G]
    w_ref, r_ref, out_ref, mg_ref = refs[2 * G:]

    @pl.when(pl.program_id(1) == 0)
    def _():
        for h in range(o_refs[0].shape[0]):
            merged = _merge_groups([o[h] for o in o_refs], [l[h] for l in l_refs])
            mg_ref[:, h * HEAD_DIM:(h + 1) * HEAD_DIM] = merged.astype(BF16)

    out_ref[...] = r_ref[...] + jnp.dot(mg_ref[...], w_ref[...], preferred_element_type=F32)


def _dil_merge_outproj(outs, lses, w, layer, res, *, bm_pref=512, bn_pref=1024):
    H, M, _ = outs[0].shape
    K = H * HEAD_DIM
    N = w.shape[2]
    bm = _pick(M, bm_pref)
    bn = _pick(N, bn_pref)
    row_spec = pl.BlockSpec((H, bm, HEAD_DIM), lambda i, n: (0, i, 0))
    pipelined = 2 * N_DIL_GROUPS * _nbytes((bm, K), F32) + _nbytes((K, bn), BF16) + 2 * _nbytes((bm, bn), F32)
    resident = _nbytes((bm, K), BF16) + 6 * _nbytes((bm, K), F32)
    return pl.pallas_call(
        _dil_merge_outproj_kernel,
        grid=(M // bm, N // bn),
        in_specs=[row_spec] * (2 * N_DIL_GROUPS) + [
            _weight_spec(layer, K, bn),
            pl.BlockSpec((bm, bn), lambda i, n: (i, n)),
        ],
        out_specs=pl.BlockSpec((bm, bn), lambda i, n: (i, n)),
        out_shape=jax.ShapeDtypeStruct((M, N), F32),
        scratch_shapes=[pltpu.VMEM((bm, K), BF16)],
        compiler_params=_compiler_params(2, pipelined, resident),
    )(*outs, *lses, w, res)


def _dil_decode_kernel(qkv_ref, w0_ref, w1_ref, w2_ref, wb_ref, sb_ref, o_ref):
    Hd = o_ref.shape[0]
    outs, lses = [], []
    for g, win_ref in enumerate((w0_ref, w1_ref, w2_ref)):
        flat = win_ref[...].reshape(DIL_STEPS * 2 * Hd, HEAD_DIM).astype(BF16)
        q = qkv_ref[3 * g]
        k_new = qkv_ref[3 * g + 1]
        v_new = qkv_ref[3 * g + 2]
        s_win = lax.dot_general(q.astype(BF16), flat, NT_DIMS, preferred_element_type=F32) * ATTN_SCALE
        s_win = s_win + wb_ref[g]
        s_new = jnp.sum(q * k_new, axis=1, keepdims=True) * ATTN_SCALE + sb_ref[g][:, :1]
        m = jnp.maximum(jnp.max(s_win, axis=1, keepdims=True), s_new)
        p_win = jnp.exp(s_win - m)
        p_new = jnp.exp(s_new - m)
        l = jnp.sum(p_win, axis=1, keepdims=True) + p_new
        p_on_v = pltpu.roll(p_win, Hd, axis=1)
        pv = jnp.dot(p_on_v.astype(BF16), flat, preferred_element_type=F32) + p_new * v_new
        outs.append(pv / l)
        lses.append(m + jnp.log(l))
    o_ref[...] = _merge_groups(outs, lses)


def _dil_decode(qkv, states, layer, win_bias, self_bias):
    DB, n_slabs, Hd, _ = qkv.shape
    views = []
    for g, st in enumerate(states):
        L = st.shape[2]
        assert L == DIL_WINDOWS[g], "window state must hold exactly one window"
        d = DIL_RATES[g]
        views.append(st.reshape(st.shape[0], DB, L // d, d * 2 * Hd, HEAD_DIM))
    win_spec = pl.BlockSpec((None, None, DIL_STEPS, 2 * Hd, HEAD_DIM), lambda b: (layer, b, 0, 0, 0))
    pipelined = 3 * _nbytes((DIL_STEPS, 2 * Hd, HEAD_DIM), F32) + _nbytes((n_slabs, Hd, HEAD_DIM), F32)
    resident = (3 * _nbytes((DIL_STEPS, 2 * Hd, HEAD_DIM), F32) + 2 * _nbytes(win_bias.shape, F32)
                + 24 * _nbytes((Hd, DIL_STEPS * 2 * Hd), F32))
    return pl.pallas_call(
        _dil_decode_kernel,
        grid=(DB,),
        in_specs=[
            pl.BlockSpec((None, n_slabs, Hd, HEAD_DIM), lambda b: (b, 0, 0, 0)),
            win_spec, win_spec, win_spec,
            pl.BlockSpec(win_bias.shape, lambda b: (0, 0, 0)),
            pl.BlockSpec(self_bias.shape, lambda b: (0, 0, 0)),
        ],
        out_specs=pl.BlockSpec((None, Hd, HEAD_DIM), lambda b: (b, 0, 0)),
        out_shape=jax.ShapeDtypeStruct((DB, Hd, HEAD_DIM), F32),
        compiler_params=_compiler_params(1, pipelined, resident),
    )(qkv, *views, win_bias, self_bias)


def _dil_decode_bias(offset_bias):
    heads = offset_bias.shape[0]
    own_k = np.arange(2 * heads)[None, :] == np.arange(heads)[:, None]
    table = jnp.where(own_k[:, None, :], offset_bias[:, :0:-1, None], MASK_VALUE)
    return table.reshape(heads, -1), jnp.broadcast_to(offset_bias[:, :1], (heads, LANES))


def _window_shift_kernel(*refs):
    G = N_DIL_GROUPS
    states, new_rows, outs, sems = refs[:G], refs[G:2 * G], refs[2 * G:3 * G], refs[3 * G]
    copies = []
    for g in range(G):
        L = states[g].shape[2]
        copies.append(pltpu.make_async_copy(
            states[g].at[:, :, pl.ds(1, L - 1)], outs[g].at[:, :, pl.ds(0, L - 1)], sems.at[2 * g]))
        copies.append(pltpu.make_async_copy(
            new_rows[g], outs[g].at[:, :, pl.ds(L - 1, 1)], sems.at[2 * g + 1]))
    for c in copies:
        c.start()
    for c in copies:
        c.wait()


def _window_shift(states, new_rows):
    any_spec = pl.BlockSpec(memory_space=pl.ANY)
    G = N_DIL_GROUPS
    return pl.pallas_call(
        _window_shift_kernel,
        in_specs=[any_spec] * (2 * G),
        out_specs=[any_spec] * G,
        out_shape=[jax.ShapeDtypeStruct(s.shape, s.dtype) for s in states],
        scratch_shapes=[pltpu.SemaphoreType.DMA((2 * G,))],
    )(*states, *new_rows)


def _pad_rows(x, rows):
    return jnp.pad(x, ((0, rows - x.shape[0]), (0, 0)))


def kernel(x_prompt, x_sample, cache_sb_k, cache_sb_v, state_win0_kv, state_win1_kv, state_win2_kv, page_table, p_prompt, p_sample, g_mix, g_ffn, g_ple, w_qkv_sb, w_o_sb, b_sb, w_qkv_dil, w_o_dil, g_qnorm_dil, g_knorm_dil, rel_bias, w_ffn_gate, w_ffn_up, w_ffn_down, w_ple_proj, w_ple_gate):
    B, S, D = x_prompt.shape
    DB, T, _ = x_sample.shape
    assert T == 1, "decode step handles one new token per sample row"
    depth = g_mix.shape[0]
    M = B * S
    sb_heads = D // HEAD_DIM
    dil_w = w_o_dil.shape[1]
    dil_heads = dil_w // HEAD_DIM
    win_states = (state_win0_kv, state_win1_kv, state_win2_kv)

    hp = x_prompt.reshape(M, D)
    hs = _pad_rows(x_sample.reshape(DB, D), SAMPLE_ROWS)

    w_qkv_sb, w_o_sb, w_qkv_dil, w_o_dil, w_ffn_gate, w_ffn_up, w_ffn_down, w_ple_proj, w_ple_gate = [
        _cast_bf16(w) for w in (w_qkv_sb, w_o_sb, w_qkv_dil, w_o_dil, w_ffn_gate, w_ffn_up, w_ffn_down,
                                w_ple_proj, w_ple_gate)]

    offset_bias = [_dil_offset_bias(rel_bias, g, dil_heads) for g in range(N_DIL_GROUPS)]
    prompt_bias = [_dil_prompt_bias(ob) for ob in offset_bias]
    win_bias, self_bias = (jnp.stack(t) for t in zip(*[_dil_decode_bias(ob) for ob in offset_bias]))

    n_sb = w_qkv_sb.shape[0]
    sb_prompt_kv = [jnp.zeros((n_sb, M, sb_heads, HEAD_DIM), F32) for _ in range(2)]
    sb_sample_kv = [jnp.zeros((n_sb, SAMPLE_ROWS, sb_heads, HEAD_DIM), F32) for _ in range(2)]
    win_p = [[] for _ in range(N_DIL_GROUPS)]
    win_s = [[] for _ in range(N_DIL_GROUPS)]

    for i in range(depth):
        j = i // 2
        if i % 2 == 0:
            k_p, v_p, qkv = _sb_qkv(hp, g_mix[i], w_qkv_sb, j, sb_prompt_kv)
            sb_prompt_kv = (k_p, v_p)
            attn = _sb_attention(qkv, b_sb[j], B, S)
            hp = _matmul_res(attn, w_o_sb, j, hp)

            k_s, v_s, qkv_s = _sb_qkv(hs, g_mix[i], w_qkv_sb, j, sb_sample_kv)
            sb_sample_kv = (k_s, v_s)
            q_s = qkv_s[0, :DB].astype(F32).reshape(DB, sb_heads, HEAD_DIM)
            attn_s = _sb_decode(q_s, cache_sb_k, cache_sb_v, j, page_table, b_sb[j])
            hs = _matmul_res(_pad_rows(attn_s.reshape(DB, D), SAMPLE_ROWS), w_o_sb, j, hs)
        else:
            ones = jnp.ones((HEAD_DIM,), F32)
            head_gain = jnp.stack([jnp.tile(gain, dil_heads)
                                   for g in range(N_DIL_GROUPS)
                                   for gain in (g_qnorm_dil[j, g], g_knorm_dil[j, g], ones)])[:, None, :]
            qkv = _dil_qkv(hp, g_mix[i], w_qkv_dil, j, head_gain)
            outs, lses = zip(*[_dil_group_attention(qkv, g, prompt_bias[g], B, S)
                               for g in range(N_DIL_GROUPS)])
            hp = _dil_merge_outproj(outs, lses, w_o_dil, j, hp)
            kv6 = qkv.reshape(N_DIL_GROUPS, 3, dil_heads, B, S, HEAD_DIM)
            for g in range(N_DIL_GROUPS):
                L = min(DIL_WINDOWS[g], S)
                win_p[g].append(jnp.transpose(kv6[g, 1:3, :, :, S - L:], (2, 3, 0, 1, 4)))

            qkv_s = _dil_qkv(hs, g_mix[i], w_qkv_dil, j, head_gain)[:, :DB]
            qkv_s = jnp.transpose(qkv_s.reshape(3 * N_DIL_GROUPS, dil_heads, DB, HEAD_DIM), (2, 0, 1, 3))
            attn_s = _dil_decode(qkv_s, win_states, j, win_bias, self_bias)
            hs = _matmul_res(_pad_rows(attn_s.reshape(DB, dil_w), SAMPLE_ROWS), w_o_dil, j, hs)
            for g in range(N_DIL_GROUPS):
                win_s[g].append(qkv_s[:, None, 3 * g + 1:3 * g + 3])

        hp = _matmul_res(_ffn_up(hp, g_ffn[i], w_ffn_gate, w_ffn_up, i), w_ffn_down, i, hp, bm_pref=1024)
        hp = _ple(hp, g_ple[i], w_ple_gate, p_prompt[i].reshape(M, -1), w_ple_proj, i)
        hs = _matmul_res(_ffn_up(hs, g_ffn[i], w_ffn_gate, w_ffn_up, i), w_ffn_down, i, hs)
        hs = _ple(hs, g_ple[i], w_ple_gate, _pad_rows(p_sample[i].reshape(DB, -1), SAMPLE_ROWS), w_ple_proj, i)

    new_states = _window_shift(win_states, [jnp.stack(rows) for rows in win_s])
    return (hp.reshape(B, S, D), hs[:DB].reshape(DB, 1, D),
            *[t.reshape(-1, B, S, sb_heads, HEAD_DIM) for t in sb_prompt_kv],
            *[t[:, :DB].reshape(-1, DB, 1, sb_heads, HEAD_DIM) for t in sb_sample_kv],
            jnp.stack(win_p[0]), jnp.stack(win_p[1]), jnp.stack(win_p[2]), *new_states)
```

```python
import functools

import numpy as np
import jax
import jax.numpy as jnp
from jax import lax
from jax.experimental import pallas as pl
from jax.experimental.pallas import tpu as pltpu

F32 = jnp.float32
BF16 = jnp.bfloat16

HEAD_DIM = 128
LANES = 128
RMS_EPS = 1e-6
ATTN_SCALE = HEAD_DIM ** -0.5
LOG2_E = float(np.log2(np.e))
DIL_WINDOWS = (128, 512, 2048)
DIL_RATES = (1, 4, 16)
DIL_STEPS = 128
N_DIL_GROUPS = 3
NUM_BUCKETS = 32
BUCKET_MAX_DIST = 2048
MASK_VALUE = -1e30
SAMPLE_ROWS = 16

V7X_VMEM_BYTES = 64 * 2**20
VMEM_CEILING = V7X_VMEM_BYTES - 8 * 2**20

NT_DIMS = (((1,), (1,)), ((), ()))


def _compiler_params(n_grid, pipelined_bytes, resident_bytes):
    need = 2 * pipelined_bytes + resident_bytes + 4 * 2**20
    return pltpu.CompilerParams(
        dimension_semantics=("arbitrary",) * n_grid,
        vmem_limit_bytes=int(min(max(need, 16 * 2**20), VMEM_CEILING)))


def _nbytes(shape, dtype):
    return int(np.prod(shape)) * jnp.dtype(dtype).itemsize


def _pick(total, preferred):
    b = min(total, preferred)
    while total % b:
        b //= 2
    return b


def _cast_kernel(w_ref, o_ref):
    o_ref[...] = w_ref[...].astype(o_ref.dtype)


def _cast_bf16(w, *, bn_pref=2048, block_bytes=4 * 2**20):
    L, K, N = w.shape
    bn = _pick(N, bn_pref)
    bk = _pick(K, block_bytes // (4 * bn))
    spec = pl.BlockSpec((None, bk, bn), lambda l, k, n: (l, k, n))
    return pl.pallas_call(
        _cast_kernel,
        grid=(L, K // bk, N // bn),
        in_specs=[spec],
        out_specs=spec,
        out_shape=jax.ShapeDtypeStruct(w.shape, BF16),
        compiler_params=_compiler_params(3, _nbytes((bk, bn), F32) + _nbytes((bk, bn), BF16), 0),
    )(w)


def _weight_spec(layer, K, bn):
    return pl.BlockSpec((None, K, bn), lambda i, n: (layer, 0, n))


def _rms_to_bf16(x_ref, g_ref, xn_ref):
    x = x_ref[...]
    ms = jnp.mean(x * x, axis=-1, keepdims=True)
    xn_ref[...] = (x * lax.rsqrt(ms + RMS_EPS) * g_ref[...]).astype(BF16)


def _qkv_kernel(x_ref, g_ref, w_ref, k_in_ref, v_in_ref, k_ref, v_ref, qkv_ref, xn_ref, *, per):
    del k_in_ref, v_in_ref
    n = pl.program_id(1)

    @pl.when(n == 0)
    def _():
        _rms_to_bf16(x_ref, g_ref, xn_ref)

    y = jnp.dot(xn_ref[...], w_ref[...], preferred_element_type=F32)
    qkv_ref[...] = y.astype(BF16)

    def by_head(ref):
        for h in range(ref.shape[1]):
            ref[:, h, :] = y[:, h * HEAD_DIM:(h + 1) * HEAD_DIM]

    @pl.when(jnp.logical_and(n >= per, n < 2 * per))
    def _():
        by_head(k_ref)

    @pl.when(n >= 2 * per)
    def _():
        by_head(v_ref)


def _sb_qkv(x, g, w, layer, kv, *, bm_pref=1024, bn_pref=1024):
    M, K = x.shape
    L, _, N = w.shape
    D = N // 3
    H = D // HEAD_DIM
    bm = _pick(M, bm_pref)
    bn = _pick(D, bn_pref)
    hb = bn // HEAD_DIM
    per = D // bn

    def kv_spec(slab):
        return pl.BlockSpec((None, bm, hb, HEAD_DIM),
                            lambda i, n: (layer, i, jnp.clip(n - slab * per, 0, per - 1), 0))

    kv_sds = jax.ShapeDtypeStruct((L, M, H, HEAD_DIM), F32)
    pipelined = _nbytes((bm, K), F32) + _nbytes((K, bn), BF16) + 2 * _nbytes((bm, bn), F32) + _nbytes((bm, bn), BF16)
    resident = _nbytes((bm, K), BF16) + 2 * _nbytes((bm, bn), F32)
    return pl.pallas_call(
        functools.partial(_qkv_kernel, per=per),
        grid=(M // bm, N // bn),
        in_specs=[
            pl.BlockSpec((bm, K), lambda i, n: (i, 0)),
            pl.BlockSpec((1, K), lambda i, n: (0, 0)),
            _weight_spec(layer, K, bn),
        ] + [pl.BlockSpec(memory_space=pl.ANY)] * 2,
        out_specs=[kv_spec(1), kv_spec(2), pl.BlockSpec((None, bm, bn), lambda i, n: (n // per, i, n % per))],
        out_shape=[kv_sds, kv_sds, jax.ShapeDtypeStruct((3, M, D), BF16)],
        input_output_aliases={3: 0, 4: 1},
        scratch_shapes=[pltpu.VMEM((bm, K), BF16)],
        compiler_params=_compiler_params(2, pipelined, resident),
    )(x, g.reshape(1, K), w, *kv)


def _dil_qkv_kernel(x_ref, g_ref, w_ref, hg_ref, o_ref, xn_ref):
    n = pl.program_id(1)

    @pl.when(n == 0)
    def _():
        _rms_to_bf16(x_ref, g_ref, xn_ref)

    y = jnp.dot(xn_ref[...], w_ref[...], preferred_element_type=F32)
    is_v = (n % 3) == 2
    heads = o_ref.shape[0]

    @pl.when(is_v)
    def _():
        for h in range(heads):
            o_ref[h] = y[:, h * HEAD_DIM:(h + 1) * HEAD_DIM]

    @pl.when(jnp.logical_not(is_v))
    def _():
        for h in range(heads):
            sl = slice(h * HEAD_DIM, (h + 1) * HEAD_DIM)
            yh = y[:, sl]
            ms = jnp.mean(yh * yh, axis=-1, keepdims=True)
            o_ref[h] = yh * lax.rsqrt(ms + RMS_EPS) * hg_ref[:, sl]


def _dil_qkv(x, g, w, layer, head_gain, *, bm_pref=1024):
    M, K = x.shape
    N = w.shape[2]
    W = head_gain.shape[2]
    heads = W // HEAD_DIM
    bm = _pick(M, bm_pref)
    pipelined = _nbytes((bm, K), F32) + _nbytes((K, W), BF16) + _nbytes((bm, W), F32)
    resident = _nbytes((bm, K), BF16) + 2 * _nbytes((bm, W), F32)
    return pl.pallas_call(
        _dil_qkv_kernel,
        grid=(M // bm, N // W),
        in_specs=[
            pl.BlockSpec((bm, K), lambda i, n: (i, 0)),
            pl.BlockSpec((1, K), lambda i, n: (0, 0)),
            _weight_spec(layer, K, W),
            pl.BlockSpec((None, 1, W), lambda i, n: (n, 0, 0)),
        ],
        out_specs=pl.BlockSpec((heads, bm, HEAD_DIM), lambda i, n: (n, i, 0)),
        out_shape=jax.ShapeDtypeStruct((N // HEAD_DIM, M, HEAD_DIM), F32),
        scratch_shapes=[pltpu.VMEM((bm, K), BF16)],
        compiler_params=_compiler_params(2, pipelined, resident),
    )(x, g.reshape(1, K), w, head_gain)


def _matmul_res_kernel(x_ref, w_ref, r_ref, o_ref):
    o_ref[...] = r_ref[...] + jnp.dot(x_ref[...].astype(BF16), w_ref[...], preferred_element_type=F32)


def _matmul_res(x, w, layer, res, *, bm_pref=512, bn_pref=512):
    M, K = x.shape
    N = w.shape[2]
    bm = _pick(M, bm_pref)
    bn = _pick(N, bn_pref)
    pipelined = _nbytes((bm, K), x.dtype) + _nbytes((K, bn), BF16) + 2 * _nbytes((bm, bn), F32)
    resident = (0 if x.dtype == BF16 else _nbytes((bm, K), BF16)) + _nbytes((bm, bn), F32)
    return pl.pallas_call(
        _matmul_res_kernel,
        grid=(M // bm, N // bn),
        in_specs=[
            pl.BlockSpec((bm, K), lambda i, n: (i, 0)),
            _weight_spec(layer, K, bn),
            pl.BlockSpec((bm, bn), lambda i, n: (i, n)),
        ],
        out_specs=pl.BlockSpec((bm, bn), lambda i, n: (i, n)),
        out_shape=jax.ShapeDtypeStruct((M, N), F32),
        compiler_params=_compiler_params(2, pipelined, resident),
    )(x, w, res)


def _ffn_up_kernel(x_ref, g_ref, wg_ref, wu_ref, o_ref, xn_ref):
    @pl.when(pl.program_id(1) == 0)
    def _():
        _rms_to_bf16(x_ref, g_ref, xn_ref)

    xn = xn_ref[...]
    a = jnp.dot(xn, wg_ref[...], preferred_element_type=F32)
    b = jnp.dot(xn, wu_ref[...], preferred_element_type=F32)
    o_ref[...] = (a * jax.nn.sigmoid(a) * b).astype(o_ref.dtype)


def _ffn_up(x, g, wg, wu, layer, *, bm_pref=1024, bn_pref=512):
    M, K = x.shape
    N = wg.shape[2]
    bm = _pick(M, bm_pref)
    bn = _pick(N, bn_pref)
    pipelined = _nbytes((bm, K), F32) + 2 * _nbytes((K, bn), BF16) + _nbytes((bm, bn), BF16)
    resident = _nbytes((bm, K), BF16) + 4 * _nbytes((bm, bn), F32)
    return pl.pallas_call(
        _ffn_up_kernel,
        grid=(M // bm, N // bn),
        in_specs=[
            pl.BlockSpec((bm, K), lambda i, n: (i, 0)),
            pl.BlockSpec((1, K), lambda i, n: (0, 0)),
            _weight_spec(layer, K, bn),
            _weight_spec(layer, K, bn),
        ],
        out_specs=pl.BlockSpec((bm, bn), lambda i, n: (i, n)),
        out_shape=jax.ShapeDtypeStruct((M, N), BF16),
        scratch_shapes=[pltpu.VMEM((bm, K), BF16)],
        compiler_params=_compiler_params(2, pipelined, resident),
    )(x, g.reshape(1, K), wg, wu)


def _ple_kernel(x_ref, g_ref, wg_ref, p_ref, wp_ref, o_ref, xn_ref):
    n = pl.program_id(1)

    @pl.when(n == 0)
    def _():
        _rms_to_bf16(x_ref, g_ref, xn_ref)

    bn = o_ref.shape[1]
    a = jnp.dot(xn_ref[...], wg_ref[...], preferred_element_type=F32)
    c = jnp.dot(p_ref[...].astype(BF16), wp_ref[...], preferred_element_type=F32)
    o_ref[...] = x_ref[:, pl.ds(pl.multiple_of(n * bn, bn), bn)] + jax.nn.sigmoid(a) * c


def _ple(x, g, w_gate, p, w_proj, layer, *, bm_pref=1024, bn_pref=512):
    M, K = x.shape
    N = w_gate.shape[2]
    P = p.shape[1]
    bm = _pick(M, bm_pref)
    bn = _pick(N, bn_pref)
    pipelined = (_nbytes((bm, K), F32) + _nbytes((K, bn), BF16) + _nbytes((bm, P), F32)
                 + _nbytes((P, bn), BF16) + _nbytes((bm, bn), F32))
    resident = _nbytes((bm, K), BF16) + 3 * _nbytes((bm, bn), F32)
    return pl.pallas_call(
        _ple_kernel,
        grid=(M // bm, N // bn),
        in_specs=[
            pl.BlockSpec((bm, K), lambda i, n: (i, 0)),
            pl.BlockSpec((1, K), lambda i, n: (0, 0)),
            _weight_spec(layer, K, bn),
            pl.BlockSpec((bm, P), lambda i, n: (i, 0)),
            _weight_spec(layer, P, bn),
        ],
        out_specs=pl.BlockSpec((bm, bn), lambda i, n: (i, n)),
        out_shape=jax.ShapeDtypeStruct((M, N), F32),
        scratch_shapes=[pltpu.VMEM((bm, K), BF16)],
        compiler_params=_compiler_params(2, pipelined, resident),
    )(x, g.reshape(1, K), w_gate, p, w_proj)


def _log2_gates(z2):
    neg_abs = pltpu.bitcast(pltpu.bitcast(z2, jnp.uint32) | jnp.uint32(0x80000000), F32)
    sp2 = jnp.log(1.0 + jnp.exp2(neg_abs)) * LOG2_E
    log_beta = jnp.minimum(z2, 0.0) - sp2
    return log_beta, log_beta - z2


def _suffix_sum(log_keep, tri):
    return jnp.dot(log_keep.astype(BF16), tri, preferred_element_type=F32)


def _sb_attn_kernel(bias_ref, q_ref, k_ref, v_ref, o_ref, acc_ref, car_ref, *, tq, heads):
    hp = pl.program_id(1)
    i = pl.program_id(2)
    row = lax.broadcasted_iota(jnp.int32, (tq, tq), 0)
    col = lax.broadcasted_iota(jnp.int32, (tq, tq), 1)
    tri = jnp.where(row > col, 1.0, 0.0).astype(BF16)
    causal = col < row
    acc_ref[...] = jnp.zeros_like(acc_ref)
    car_ref[...] = jnp.zeros_like(car_ref)

    def visit(j, mask):
        koff = pl.multiple_of(j * tq, tq)
        lanes = [slice(g * HEAD_DIM, (g + 1) * HEAD_DIM) for g in range(heads)]
        logits = [lax.dot_general(q_ref[:, sl], k_ref[pl.ds(koff, tq), sl], NT_DIMS,
                                  preferred_element_type=F32) for sl in lanes]
        gates = []
        for g, z in enumerate(logits):
            log_beta, log_keep = _log2_gates(z * (ATTN_SCALE * LOG2_E) + bias_ref[hp * heads + g] * LOG2_E)
            if mask is not None:
                log_keep = jnp.where(mask, log_keep, 0.0)
            gates.append((log_beta, log_keep))
        later = [_suffix_sum(log_keep, tri) for _, log_keep in gates]
        weights = []
        for sl, (log_beta, log_keep), lt in zip(lanes, gates, later):
            carry = car_ref[:, sl]
            w = jnp.exp2(log_beta + lt + jnp.tile(carry, (1, tq // LANES)))
            if mask is not None:
                w = jnp.where(mask, w, 0.0)
            weights.append(w.astype(BF16))
            car_ref[:, sl] = carry + jnp.broadcast_to(jnp.sum(log_keep, axis=1, keepdims=True), carry.shape)
        for sl, w in zip(lanes, weights):
            acc_ref[:, sl] += jnp.dot(w, v_ref[pl.ds(koff, tq), sl], preferred_element_type=F32)

    visit(i, causal)

    def body(jj, c):
        visit(i - 1 - jj, None)
        return c

    lax.fori_loop(0, i, body, 0)
    o_ref[...] = acc_ref[...].astype(o_ref.dtype)


def _sb_attention(qkv, bias, B, S, *, tq_pref=256, heads_pref=8):
    _, M, W = qkv.shape
    tq = _pick(S, tq_pref)
    heads = _pick(W // HEAD_DIM, heads_pref)
    hw = heads * HEAD_DIM
    qkv4 = qkv.reshape(3, B, S, W)

    def kv_spec(slab):
        return pl.BlockSpec((None, None, S, hw), lambda b, h, i: (slab, b, 0, h), pipeline_mode=pl.Buffered(1))

    pipelined = 2 * _nbytes((tq, hw), BF16)
    resident = 2 * _nbytes((S, hw), BF16) + 2 * _nbytes((tq, hw), F32) + 6 * heads * _nbytes((tq, tq), F32)
    out = pl.pallas_call(
        functools.partial(_sb_attn_kernel, tq=tq, heads=heads),
        grid=(B, W // hw, S // tq),
        in_specs=[
            pl.BlockSpec(memory_space=pltpu.SMEM),
            pl.BlockSpec((None, None, tq, hw), lambda b, h, i: (0, b, i, h)),
            kv_spec(1),
            kv_spec(2),
        ],
        out_specs=pl.BlockSpec((None, tq, hw), lambda b, h, i: (b, i, h)),
        out_shape=jax.ShapeDtypeStruct((B, S, W), BF16),
        scratch_shapes=[pltpu.VMEM((tq, hw), F32), pltpu.VMEM((tq, hw), F32)],
        compiler_params=_compiler_params(3, pipelined, resident),
    )(bias, qkv4, qkv4, qkv4)
    return out.reshape(M, W)


def _sb_decode_kernel(pt_ref, q_ref, bias_ref, *refs, pages, heads):
    del pt_ref
    k_refs, v_refs = refs[:pages], refs[pages:2 * pages]
    o_ref, acc_ref, car_ref = refs[2 * pages:]
    p = pl.program_id(1)
    H = heads
    rows = k_refs[0].shape[0]
    chunks = rows // LANES
    SR = chunks * H

    @pl.when(p == 0)
    def _():
        acc_ref[...] = jnp.zeros_like(acc_ref)
        car_ref[...] = jnp.zeros_like(car_ref)

    own_head = (lax.broadcasted_iota(jnp.int32, (SR, LANES), 1) % H
                == lax.broadcasted_iota(jnp.int32, (SR, LANES), 0) % H)
    src = lax.broadcasted_iota(jnp.int32, (LANES, LANES), 0)
    dst = lax.broadcasted_iota(jnp.int32, (LANES, LANES), 1)
    tri = jnp.where((src // H > dst // H) & (src % H == dst % H), 1.0, 0.0).astype(BF16)
    q = q_ref[...].astype(BF16)

    scores = [lax.dot_general(q, k_ref[...].astype(BF16), NT_DIMS, preferred_element_type=F32)
              for k_ref in k_refs]
    gates = []
    for sc in scores:
        z = jnp.concatenate([sc[:, c * LANES:(c + 1) * LANES] for c in range(chunks)], axis=0)
        log_beta, log_keep = _log2_gates(z * (ATTN_SCALE * LOG2_E) + bias_ref[...])
        gates.append((log_beta, jnp.where(own_head, log_keep, 0.0)))
    within = [_suffix_sum(log_keep, tri) for _, log_keep in gates]
    run = car_ref[:, :1]
    weights = []
    for (log_beta, log_keep), inside in zip(gates, within):
        chunk_keep = jnp.sum(log_keep, axis=1, keepdims=True)
        after = [None] * chunks
        for c in reversed(range(chunks)):
            after[c] = run
            run = run + chunk_keep[c * H:(c + 1) * H]
        w = jnp.where(own_head, jnp.exp2(log_beta + inside + jnp.concatenate(after, axis=0)), 0.0)
        weights.append(jnp.concatenate([w[c * H:(c + 1) * H] for c in range(chunks)], axis=1).astype(BF16))
    car_ref[...] = jnp.broadcast_to(run, car_ref.shape)
    acc = acc_ref[...]
    for wm, v_ref in zip(weights, v_refs):
        acc = acc + jnp.dot(wm, v_ref[...].astype(BF16), preferred_element_type=F32)
    acc_ref[...] = acc

    @pl.when(p == pl.num_programs(1) - 1)
    def _():
        o_ref[...] = acc_ref[...]


def _sb_decode(q, cache_k, cache_v, layer, page_table, bias, *, pages_pref=4):
    DB, H, _ = q.shape
    n_layers, n_pool, page = cache_k.shape[:3]
    rows = page * H
    chunks = rows // LANES
    n_pages = page_table.shape[1]
    pages = _pick(n_pages, pages_pref)
    flat_k = cache_k.reshape(n_layers, n_pool, rows, HEAD_DIM)
    flat_v = cache_v.reshape(n_layers, n_pool, rows, HEAD_DIM)

    def page_spec(s):
        def index(b, p, pt):
            return (layer, pt[b, n_pages - 1 - (p * pages + s)], 0, 0)
        return pl.BlockSpec((None, None, rows, HEAD_DIM), index)

    pipelined = 2 * pages * _nbytes((rows, HEAD_DIM), F32)
    resident = pages * (2 * _nbytes((rows, HEAD_DIM), BF16) + 12 * _nbytes((chunks * H, LANES), F32))
    grid_spec = pltpu.PrefetchScalarGridSpec(
        num_scalar_prefetch=1,
        grid=(DB, n_pages // pages),
        in_specs=[
            pl.BlockSpec((None, H, HEAD_DIM), lambda b, p, pt: (b, 0, 0)),
            pl.BlockSpec((chunks * H, LANES), lambda b, p, pt: (0, 0)),
        ] + [page_spec(s) for s in range(pages)] * 2,
        out_specs=pl.BlockSpec((None, H, HEAD_DIM), lambda b, p, pt: (b, 0, 0)),
        scratch_shapes=[pltpu.VMEM((H, HEAD_DIM), F32), pltpu.VMEM((H, LANES), F32)],
    )
    return pl.pallas_call(
        functools.partial(_sb_decode_kernel, pages=pages, heads=H),
        grid_spec=grid_spec,
        out_shape=jax.ShapeDtypeStruct((DB, H, HEAD_DIM), F32),
        compiler_params=_compiler_params(2, pipelined, resident),
    )(page_table, q, jnp.tile(bias[:, None] * LOG2_E, (chunks, LANES)), *([flat_k] * pages), *([flat_v] * pages))


def _t5_bucket(dist):
    dist = np.asarray(dist, np.int64)
    max_exact = NUM_BUCKETS // 2
    large = max_exact + (np.log(np.maximum(dist, 1) / max_exact) / np.log(BUCKET_MAX_DIST / max_exact)
                         * (NUM_BUCKETS - max_exact)).astype(np.int64)
    large = np.minimum(large, NUM_BUCKETS - 1)
    return np.where(dist < max_exact, dist, large).astype(np.int32)


def _dil_offset_bias(rel_bias, g, heads):
    buckets = _t5_bucket(DIL_RATES[g] * np.arange(DIL_STEPS + 1))
    return rel_bias[g * heads:(g + 1) * heads][:, buckets].astype(F32)


def _dil_prompt_bias(offset_bias):
    heads = offset_bias.shape[0]
    S = DIL_STEPS
    by_shift = jnp.concatenate([jnp.full((heads, S - 1), MASK_VALUE, F32), offset_bias[:, ::-1],
                                jnp.full((heads, S), MASK_VALUE, F32)], axis=1)
    skew = jnp.tile(by_shift, (1, S))[:, :S * (3 * S - 1)].reshape(heads, S, 3 * S - 1)
    general = skew[:, :, S - 1:3 * S - 1]
    first = jnp.where(np.arange(2 * S)[None, None, :] >= S, general, MASK_VALUE)
    return jnp.stack([first, general])


def _dil_attn_kernel(q_ref, kc_ref, kp_ref, vc_ref, vp_ref, bias_ref, o_ref, l_ref, *, rate, unroll):
    not_first = jnp.minimum(pl.program_id(1), 1)
    heads = q_ref.shape[0]
    head0 = pl.program_id(2) * heads

    def body(t, c):
        tiles = []
        for u in range(unroll):
            r = t * unroll + u
            rows = pl.ds(r, DIL_STEPS, stride=rate) if rate > 1 else pl.ds(0, DIL_STEPS)
            tiles += [(h, rows) for h in range(heads)]
        scores = []
        for h, rows in tiles:
            keys = jnp.concatenate([kp_ref[h, rows, :], kc_ref[h, rows, :]], axis=0).astype(BF16)
            scores.append(lax.dot_general(q_ref[h, rows, :].astype(BF16), keys, NT_DIMS,
                                          preferred_element_type=F32))
        probs = []
        for (h, rows), s in zip(tiles, scores):
            s = s * ATTN_SCALE + bias_ref[not_first, head0 + h]
            m = jnp.max(s, axis=1, keepdims=True)
            p = jnp.exp(s - m)
            l = jnp.sum(p, axis=1, keepdims=True)
            l_ref[h, rows, :] = jnp.broadcast_to(m + jnp.log(l), (DIL_STEPS, HEAD_DIM))
            probs.append((p.astype(BF16), l))
        for (h, rows), (p, l) in zip(tiles, probs):
            vals = jnp.concatenate([vp_ref[h, rows, :], vc_ref[h, rows, :]], axis=0).astype(BF16)
            o_ref[h, rows, :] = jnp.dot(p, vals, preferred_element_type=F32) / l
        return c

    lax.fori_loop(0, rate // unroll, body, 0)


def _dil_group_attention(qkv, g, bias, B, S, *, block_bytes=2 * 2**20):
    n_slabs, M, _ = qkv.shape
    H = n_slabs // (3 * N_DIL_GROUPS)
    d = DIL_RATES[g]
    chunk = DIL_STEPS * d
    assert S % chunk == 0, "sequence must be a whole number of dilation chunks"
    nc = S // chunk
    hb = _pick(H, max(block_bytes // _nbytes((chunk, HEAD_DIM), F32), 1))
    blk = (hb, chunk, HEAD_DIM)

    def spec(slab, prev):
        base = slab * (H // hb)
        if prev:
            return pl.BlockSpec(blk, lambda b, c, h: (base + h, b * nc + jnp.maximum(c - 1, 0), 0))
        return pl.BlockSpec(blk, lambda b, c, h: (base + h, b * nc + c, 0))

    out_spec = pl.BlockSpec(blk, lambda b, c, h: (h, b * nc + c, 0))
    out_sds = jax.ShapeDtypeStruct((H, M, HEAD_DIM), F32)
    unroll = max(1, min(d, 8 // hb))
    pipelined = 7 * _nbytes(blk, F32)
    resident = 2 * _nbytes(bias.shape, F32) + 16 * unroll * hb * _nbytes((DIL_STEPS, 2 * DIL_STEPS), F32)
    return pl.pallas_call(
        functools.partial(_dil_attn_kernel, rate=d, unroll=unroll),
        grid=(B, nc, H // hb),
        in_specs=[
            spec(3 * g, False), spec(3 * g + 1, False), spec(3 * g + 1, True),
            spec(3 * g + 2, False), spec(3 * g + 2, True),
            pl.BlockSpec(bias.shape, lambda b, c, h: (0, 0, 0, 0)),
        ],
        out_specs=[out_spec, out_spec],
        out_shape=[out_sds, out_sds],
        compiler_params=_compiler_params(3, pipelined, resident),
    )(qkv, qkv, qkv, qkv, qkv, bias)


def _merge_groups(outs, lses):
    m = functools.reduce(jnp.maximum, lses)
    es = [jnp.exp(l - m) for l in lses]
    den = functools.reduce(jnp.add, es)
    num = functools.reduce(jnp.add, [e * o for e, o in zip(es, outs)])
    return num / den


def _dil_merge_outproj_kernel(*refs):
    G = N_DIL_GROUPS
    o_refs, l_refs = refs[:G], refs[G:2 * G]
    w_ref, r_ref, out_ref, mg_ref = refs[2 * G:]

    @pl.when(pl.program_id(1) == 0)
    def _():
        for h in range(o_refs[0].shape[0]):
            merged = _merge_groups([o[h] for o in o_refs], [l[h] for l in l_refs])
            mg_ref[:, h * HEAD_DIM:(h + 1) * HEAD_DIM] = merged.astype(BF16)

    out_ref[...] = r_ref[...] + jnp.dot(mg_ref[...], w_ref[...], preferred_element_type=F32)


def _dil_merge_outproj(outs, lses, w, layer, res, *, bm_pref=512, bn_pref=1024):
    H, M, _ = outs[0].shape
    K = H * HEAD_DIM
    N = w.shape[2]
    bm = _pick(M, bm_pref)
    bn = _pick(N, bn_pref)
    row_spec = pl.BlockSpec((H, bm, HEAD_DIM), lambda i, n: (0, i, 0))
    pipelined = 2 * N_DIL_GROUPS * _nbytes((bm, K), F32) + _nbytes((K, bn), BF16) + 2 * _nbytes((bm, bn), F32)
    resident = _nbytes((bm, K), BF16) + 6 * _nbytes((bm, K), F32)
    return pl.pallas_call(
        _dil_merge_outproj_kernel,
        grid=(M // bm, N // bn),
        in_specs=[row_spec] * (2 * N_DIL_GROUPS) + [
            _weight_spec(layer, K, bn),
            pl.BlockSpec((bm, bn), lambda i, n: (i, n)),
        ],
        out_specs=pl.BlockSpec((bm, bn), lambda i, n: (i, n)),
        out_shape=jax.ShapeDtypeStruct((M, N), F32),
        scratch_shapes=[pltpu.VMEM((bm, K), BF16)],
        compiler_params=_compiler_params(2, pipelined, resident),
    )(*outs, *lses, w, res)


def _dil_decode_kernel(qkv_ref, w0_ref, w1_ref, w2_ref, wb_ref, sb_ref, o_ref):
    Hd = o_ref.shape[0]
    outs, lses = [], []
    for g, win_ref in enumerate((w0_ref, w1_ref, w2_ref)):
        flat = win_ref[...].reshape(DIL_STEPS * 2 * Hd, HEAD_DIM).astype(BF16)
        q = qkv_ref[3 * g]
        k_new = qkv_ref[3 * g + 1]
        v_new = qkv_ref[3 * g + 2]
        s_win = lax.dot_general(q.astype(BF16), flat, NT_DIMS, preferred_element_type=F32) * ATTN_SCALE
        s_win = s_win + wb_ref[g]
        s_new = jnp.sum(q * k_new, axis=1, keepdims=True) * ATTN_SCALE + sb_ref[g][:, :1]
        m = jnp.maximum(jnp.max(s_win, axis=1, keepdims=True), s_new)
        p_win = jnp.exp(s_win - m)
        p_new = jnp.exp(s_new - m)
        l = jnp.sum(p_win, axis=1, keepdims=True) + p_new
        p_on_v = pltpu.roll(p_win, Hd, axis=1)
        pv = jnp.dot(p_on_v.astype(BF16), flat, preferred_element_type=F32) + p_new * v_new
        outs.append(pv / l)
        lses.append(m + jnp.log(l))
    o_ref[...] = _merge_groups(outs, lses)


def _dil_decode(qkv, states, layer, win_bias, self_bias):
    DB, n_slabs, Hd, _ = qkv.shape
    views = []
    for g, st in enumerate(states):
        L = st.shape[2]
        assert L == DIL_WINDOWS[g], "window state must hold exactly one window"
        d = DIL_RATES[g]
        views.append(st.reshape(st.shape[0], DB, L // d, d * 2 * Hd, HEAD_DIM))
    win_spec = pl.BlockSpec((None, None, DIL_STEPS, 2 * Hd, HEAD_DIM), lambda b: (layer, b, 0, 0, 0))
    pipelined = 3 * _nbytes((DIL_STEPS, 2 * Hd, HEAD_DIM), F32) + _nbytes((n_slabs, Hd, HEAD_DIM), F32)
    resident = (3 * _nbytes((DIL_STEPS, 2 * Hd, HEAD_DIM), F32) + 2 * _nbytes(win_bias.shape, F32)
                + 24 * _nbytes((Hd, DIL_STEPS * 2 * Hd), F32))
    return pl.pallas_call(
        _dil_decode_kernel,
        grid=(DB,),
        in_specs=[
            pl.BlockSpec((None, n_slabs, Hd, HEAD_DIM), lambda b: (b, 0, 0, 0)),
            win_spec, win_spec, win_spec,
            pl.BlockSpec(win_bias.shape, lambda b: (0, 0, 0)),
            pl.BlockSpec(self_bias.shape, lambda b: (0, 0, 0)),
        ],
        out_specs=pl.BlockSpec((None, Hd, HEAD_DIM), lambda b: (b, 0, 0)),
        out_shape=jax.ShapeDtypeStruct((DB, Hd, HEAD_DIM), F32),
        compiler_params=_compiler_params(1, pipelined, resident),
    )(qkv, *views, win_bias, self_bias)


def _dil_decode_bias(offset_bias):
    heads = offset_bias.shape[0]
    own_k = np.arange(2 * heads)[None, :] == np.arange(heads)[:, None]
    table = jnp.where(own_k[:, None, :], offset_bias[:, :0:-1, None], MASK_VALUE)
    return table.reshape(heads, -1), jnp.broadcast_to(offset_bias[:, :1], (heads, LANES))


def _window_shift_kernel(cur_ref, nxt_ref, new_ref, o_ref):
    R = o_ref.shape[0]
    last = pl.program_id(1) == pl.num_programs(1) - 1
    if R > 1:
        o_ref[0:R - 1] = cur_ref[1:R]
    o_ref[R - 1] = jnp.where(last, new_ref[0], nxt_ref[0])


def _window_shift(state, new_rows, *, block_bytes=2 * 2**20):
    n_layers, DB, L, two, Hd, _ = state.shape
    row = (two * Hd, HEAD_DIM)
    R = _pick(L, max(block_bytes // _nbytes(row, F32), 1))
    flat = state.reshape(n_layers * DB, L, *row)
    out = pl.pallas_call(
        _window_shift_kernel,
        grid=(n_layers * DB, L // R),
        in_specs=[
            pl.BlockSpec((None, R, *row), lambda s, c: (s, c, 0, 0)),
            pl.BlockSpec((None, 1, *row), lambda s, c: (s, jnp.minimum((c + 1) * R, L - 1), 0, 0)),
            pl.BlockSpec((None, 1, *row), lambda s, c: (s, 0, 0, 0)),
        ],
        out_specs=pl.BlockSpec((None, R, *row), lambda s, c: (s, c, 0, 0)),
        out_shape=jax.ShapeDtypeStruct(flat.shape, state.dtype),
        compiler_params=_compiler_params(2, 2 * _nbytes((R, *row), F32), 0),
    )(flat, flat, new_rows.reshape(n_layers * DB, 1, *row))
    return out.reshape(state.shape)


def _pad_rows(x, rows):
    return jnp.pad(x, ((0, rows - x.shape[0]), (0, 0)))


def kernel(x_prompt, x_sample, cache_sb_k, cache_sb_v, state_win0_kv, state_win1_kv, state_win2_kv, page_table, p_prompt, p_sample, g_mix, g_ffn, g_ple, w_qkv_sb, w_o_sb, b_sb, w_qkv_dil, w_o_dil, g_qnorm_dil, g_knorm_dil, rel_bias, w_ffn_gate, w_ffn_up, w_ffn_down, w_ple_proj, w_ple_gate):
    B, S, D = x_prompt.shape
    DB, T, _ = x_sample.shape
    assert T == 1, "decode step handles one new token per sample row"
    depth = g_mix.shape[0]
    M = B * S
    sb_heads = D // HEAD_DIM
    dil_w = w_o_dil.shape[1]
    dil_heads = dil_w // HEAD_DIM
    win_states = (state_win0_kv, state_win1_kv, state_win2_kv)

    hp = x_prompt.reshape(M, D)
    hs = _pad_rows(x_sample.reshape(DB, D), SAMPLE_ROWS)

    w_qkv_sb, w_o_sb, w_qkv_dil, w_o_dil, w_ffn_gate, w_ffn_up, w_ffn_down, w_ple_proj, w_ple_gate = [
        _cast_bf16(w) for w in (w_qkv_sb, w_o_sb, w_qkv_dil, w_o_dil, w_ffn_gate, w_ffn_up, w_ffn_down,
                                w_ple_proj, w_ple_gate)]

    offset_bias = [_dil_offset_bias(rel_bias, g, dil_heads) for g in range(N_DIL_GROUPS)]
    prompt_bias = [_dil_prompt_bias(ob) for ob in offset_bias]
    win_bias, self_bias = (jnp.stack(t) for t in zip(*[_dil_decode_bias(ob) for ob in offset_bias]))

    n_sb = w_qkv_sb.shape[0]
    sb_prompt_kv = [jnp.zeros((n_sb, M, sb_heads, HEAD_DIM), F32) for _ in range(2)]
    sb_sample_kv = [jnp.zeros((n_sb, SAMPLE_ROWS, sb_heads, HEAD_DIM), F32) for _ in range(2)]
    win_p = [[] for _ in range(N_DIL_GROUPS)]
    win_s = [[] for _ in range(N_DIL_GROUPS)]

    for i in range(depth):
        j = i // 2
        if i % 2 == 0:
            k_p, v_p, qkv = _sb_qkv(hp, g_mix[i], w_qkv_sb, j, sb_prompt_kv)
            sb_prompt_kv = (k_p, v_p)
            attn = _sb_attention(qkv, b_sb[j], B, S)
            hp = _matmul_res(attn, w_o_sb, j, hp)

            k_s, v_s, qkv_s = _sb_qkv(hs, g_mix[i], w_qkv_sb, j, sb_sample_kv)
            sb_sample_kv = (k_s, v_s)
            q_s = qkv_s[0, :DB].astype(F32).reshape(DB, sb_heads, HEAD_DIM)
            attn_s = _sb_decode(q_s, cache_sb_k, cache_sb_v, j, page_table, b_sb[j])
            hs = _matmul_res(_pad_rows(attn_s.reshape(DB, D), SAMPLE_ROWS), w_o_sb, j, hs)
        else:
            ones = jnp.ones((HEAD_DIM,), F32)
            head_gain = jnp.stack([jnp.tile(gain, dil_heads)
                                   for g in range(N_DIL_GROUPS)
                                   for gain in (g_qnorm_dil[j, g], g_knorm_dil[j, g], ones)])[:, None, :]
            qkv = _dil_qkv(hp, g_mix[i], w_qkv_dil, j, head_gain)
            outs, lses = zip(*[_dil_group_attention(qkv, g, prompt_bias[g], B, S)
                               for g in range(N_DIL_GROUPS)])
            hp = _dil_merge_outproj(outs, lses, w_o_dil, j, hp)
            kv6 = qkv.reshape(N_DIL_GROUPS, 3, dil_heads, B, S, HEAD_DIM)
            for g in range(N_DIL_GROUPS):
                L = min(DIL_WINDOWS[g], S)
                win_p[g].append(jnp.transpose(kv6[g, 1:3, :, :, S - L:], (2, 3, 0, 1, 4)))

            qkv_s = _dil_qkv(hs, g_mix[i], w_qkv_dil, j, head_gain)[:, :DB]
            qkv_s = jnp.transpose(qkv_s.reshape(3 * N_DIL_GROUPS, dil_heads, DB, HEAD_DIM), (2, 0, 1, 3))
            attn_s = _dil_decode(qkv_s, win_states, j, win_bias, self_bias)
            hs = _matmul_res(_pad_rows(attn_s.reshape(DB, dil_w), SAMPLE_ROWS), w_o_dil, j, hs)
            for g in range(N_DIL_GROUPS):
                win_s[g].append(qkv_s[:, None, 3 * g + 1:3 * g + 3])

        hp = _matmul_res(_ffn_up(hp, g_ffn[i], w_ffn_gate, w_ffn_up, i), w_ffn_down, i, hp, bm_pref=1024)
        hp = _ple(hp, g_ple[i], w_ple_gate, p_prompt[i].reshape(M, -1), w_ple_proj, i)
        hs = _matmul_res(_ffn_up(hs, g_ffn[i], w_ffn_gate, w_ffn_up, i), w_ffn_down, i, hs)
        hs = _ple(hs, g_ple[i], w_ple_gate, _pad_rows(p_sample[i].reshape(DB, -1), SAMPLE_ROWS), w_ple_proj, i)

    new_states = [_window_shift(st, jnp.stack(rows)) for st, rows in zip(win_states, win_s)]
    return (hp.reshape(B, S, D), hs[:DB].reshape(DB, 1, D),
            *[t.reshape(-1, B, S, sb_heads, HEAD_DIM) for t in sb_prompt_kv],
            *[t[:, :DB].reshape(-1, DB, 1, sb_heads, HEAD_DIM) for t in sb_sample_kv],
            jnp.stack(win_p[0]), jnp.stack(win_p[1]), jnp.stack(win_p[2]), *new_states)
```

```python
import functools

import numpy as np
import jax
import jax.numpy as jnp
from jax import lax
from jax.experimental import pallas as pl
from jax.experimental.pallas import tpu as pltpu

F32 = jnp.float32
BF16 = jnp.bfloat16

HEAD_DIM = 128
LANES = 128
RMS_EPS = 1e-6
ATTN_SCALE = HEAD_DIM ** -0.5
LOG2_E = float(np.log2(np.e))
DIL_WINDOWS = (128, 512, 2048)
DIL_RATES = (1, 4, 16)
DIL_STEPS = 128
N_DIL_GROUPS = 3
NUM_BUCKETS = 32
BUCKET_MAX_DIST = 2048
MASK_VALUE = -1e30
SAMPLE_ROWS = 16

V7X_VMEM_BYTES = 64 * 2**20
VMEM_CEILING = V7X_VMEM_BYTES - 8 * 2**20

NT_DIMS = (((1,), (1,)), ((), ()))


def _compiler_params(n_grid, pipelined_bytes, resident_bytes):
    need = 2 * pipelined_bytes + resident_bytes + 4 * 2**20
    return pltpu.CompilerParams(
        dimension_semantics=("arbitrary",) * n_grid,
        vmem_limit_bytes=int(min(max(need, 16 * 2**20), VMEM_CEILING)))


def _nbytes(shape, dtype):
    return int(np.prod(shape)) * jnp.dtype(dtype).itemsize


def _pick(total, preferred):
    b = min(total, preferred)
    while total % b:
        b //= 2
    return b


def _cast_kernel(w_ref, o_ref):
    o_ref[...] = w_ref[...].astype(o_ref.dtype)


def _cast_bf16(w, *, bn_pref=2048, block_bytes=4 * 2**20):
    L, K, N = w.shape
    bn = _pick(N, bn_pref)
    bk = _pick(K, block_bytes // (4 * bn))
    spec = pl.BlockSpec((None, bk, bn), lambda l, k, n: (l, k, n))
    return pl.pallas_call(
        _cast_kernel,
        grid=(L, K // bk, N // bn),
        in_specs=[spec],
        out_specs=spec,
        out_shape=jax.ShapeDtypeStruct(w.shape, BF16),
        compiler_params=_compiler_params(3, _nbytes((bk, bn), F32) + _nbytes((bk, bn), BF16), 0),
    )(w)


def _weight_spec(layer, K, bn):
    return pl.BlockSpec((None, K, bn), lambda i, n: (layer, 0, n))


def _row_chunks(rows, preferred=256):
    c = _pick(rows, preferred)
    return [slice(r, r + c) for r in range(0, rows, c)]


def _rms_to_bf16(x_ref, g_ref, xn_ref):
    x = x_ref[...]
    ms = jnp.mean(x * x, axis=-1, keepdims=True)
    xn_ref[...] = (x * lax.rsqrt(ms + RMS_EPS) * g_ref[...]).astype(BF16)


def _qkv_kernel(x_ref, g_ref, w_ref, k_in_ref, v_in_ref, k_ref, v_ref, qkv_ref, xn_ref, *, per):
    del k_in_ref, v_in_ref
    n = pl.program_id(1)

    @pl.when(n == 0)
    def _():
        _rms_to_bf16(x_ref, g_ref, xn_ref)

    def slab(store):
        chunks = _row_chunks(xn_ref.shape[0])
        ys = [jnp.dot(xn_ref[rs, :], w_ref[...], preferred_element_type=F32) for rs in chunks]
        for rs, y in zip(chunks, ys):
            store(rs, y)

    @pl.when(n < per)
    def _():
        def store(rs, y):
            qkv_ref[rs, :] = (y * (ATTN_SCALE * LOG2_E)).astype(BF16)
        slab(store)

    for first, kv_ref in ((per, k_ref), (2 * per, v_ref)):
        @pl.when(jnp.logical_and(n >= first, n < first + per))
        def _():
            def store(rs, y):
                qkv_ref[rs, :] = y.astype(BF16)
                kv_ref[rs] = y.reshape((y.shape[0],) + kv_ref.shape[1:])
            slab(store)


def _sb_qkv(x, g, w, layer, kv, *, bm_pref=1024, bn_pref=1024):
    M, K = x.shape
    L, _, N = w.shape
    D = N // 3
    H = D // HEAD_DIM
    bm = _pick(M, bm_pref)
    bn = _pick(D, bn_pref)
    hb = bn // HEAD_DIM
    per = D // bn

    def kv_spec(slab):
        return pl.BlockSpec((None, bm, hb, HEAD_DIM),
                            lambda i, n: (layer, i, jnp.clip(n - slab * per, 0, per - 1), 0))

    kv_sds = jax.ShapeDtypeStruct((L, M, H, HEAD_DIM), F32)
    pipelined = _nbytes((bm, K), F32) + _nbytes((K, bn), BF16) + 2 * _nbytes((bm, bn), F32) + _nbytes((bm, bn), BF16)
    resident = _nbytes((bm, K), BF16) + 2 * _nbytes((bm, bn), F32)
    return pl.pallas_call(
        functools.partial(_qkv_kernel, per=per),
        grid=(M // bm, N // bn),
        in_specs=[
            pl.BlockSpec((bm, K), lambda i, n: (i, 0)),
            pl.BlockSpec((1, K), lambda i, n: (0, 0)),
            _weight_spec(layer, K, bn),
        ] + [pl.BlockSpec(memory_space=pl.ANY)] * 2,
        out_specs=[kv_spec(1), kv_spec(2), pl.BlockSpec((None, bm, bn), lambda i, n: (n // per, i, n % per))],
        out_shape=[kv_sds, kv_sds, jax.ShapeDtypeStruct((3, M, D), BF16)],
        input_output_aliases={3: 0, 4: 1},
        scratch_shapes=[pltpu.VMEM((bm, K), BF16)],
        compiler_params=_compiler_params(2, pipelined, resident),
    )(x, g.reshape(1, K), w, *kv)


def _dil_qkv_kernel(x_ref, g_ref, w_ref, hg_ref, o_ref, xn_ref):
    n = pl.program_id(1)

    @pl.when(n == 0)
    def _():
        _rms_to_bf16(x_ref, g_ref, xn_ref)

    is_v = (n % 3) == 2
    chunks = _row_chunks(xn_ref.shape[0])
    ys = [jnp.dot(xn_ref[rs, :], w_ref[...], preferred_element_type=F32) for rs in chunks]
    for rs, y in zip(chunks, ys):
        for h in range(o_ref.shape[0]):
            sl = slice(h * HEAD_DIM, (h + 1) * HEAD_DIM)
            yh = y[:, sl]
            ms = jnp.mean(yh * yh, axis=-1, keepdims=True)
            o_ref[h, rs, :] = jnp.where(is_v, yh, yh * lax.rsqrt(ms + RMS_EPS) * hg_ref[:, sl])


def _dil_qkv(x, g, w, layer, head_gain, *, bm_pref=1024):
    M, K = x.shape
    N = w.shape[2]
    W = head_gain.shape[2]
    heads = W // HEAD_DIM
    bm = _pick(M, bm_pref)
    pipelined = _nbytes((bm, K), F32) + _nbytes((K, W), BF16) + _nbytes((bm, W), F32)
    resident = _nbytes((bm, K), BF16) + 2 * _nbytes((bm, W), F32)
    return pl.pallas_call(
        _dil_qkv_kernel,
        grid=(M // bm, N // W),
        in_specs=[
            pl.BlockSpec((bm, K), lambda i, n: (i, 0)),
            pl.BlockSpec((1, K), lambda i, n: (0, 0)),
            _weight_spec(layer, K, W),
            pl.BlockSpec((None, 1, W), lambda i, n: (n, 0, 0)),
        ],
        out_specs=pl.BlockSpec((heads, bm, HEAD_DIM), lambda i, n: (n, i, 0)),
        out_shape=jax.ShapeDtypeStruct((N // HEAD_DIM, M, HEAD_DIM), F32),
        scratch_shapes=[pltpu.VMEM((bm, K), BF16)],
        compiler_params=_compiler_params(2, pipelined, resident),
    )(x, g.reshape(1, K), w, head_gain)


def _matmul_res_kernel(x_ref, w_ref, r_ref, o_ref):
    o_ref[...] = r_ref[...] + jnp.dot(x_ref[...].astype(BF16), w_ref[...], preferred_element_type=F32)


def _matmul_res(x, w, layer, res, *, bm_pref=512, bn_pref=512):
    M, K = x.shape
    N = w.shape[2]
    bm = _pick(M, bm_pref)
    bn = _pick(N, bn_pref)
    pipelined = _nbytes((bm, K), x.dtype) + _nbytes((K, bn), BF16) + 2 * _nbytes((bm, bn), F32)
    resident = (0 if x.dtype == BF16 else _nbytes((bm, K), BF16)) + _nbytes((bm, bn), F32)
    return pl.pallas_call(
        _matmul_res_kernel,
        grid=(M // bm, N // bn),
        in_specs=[
            pl.BlockSpec((bm, K), lambda i, n: (i, 0)),
            _weight_spec(layer, K, bn),
            pl.BlockSpec((bm, bn), lambda i, n: (i, n)),
        ],
        out_specs=pl.BlockSpec((bm, bn), lambda i, n: (i, n)),
        out_shape=jax.ShapeDtypeStruct((M, N), F32),
        compiler_params=_compiler_params(2, pipelined, resident),
    )(x, w, res)


def _ffn_up_kernel(x_ref, g_ref, wg_ref, wu_ref, o_ref, xn_ref):
    @pl.when(pl.program_id(1) == 0)
    def _():
        _rms_to_bf16(x_ref, g_ref, xn_ref)

    chunks = _row_chunks(xn_ref.shape[0])
    gate = [jnp.dot(xn_ref[rs, :], wg_ref[...], preferred_element_type=F32) for rs in chunks]
    up = [jnp.dot(xn_ref[rs, :], wu_ref[...], preferred_element_type=F32) for rs in chunks]
    for rs, a, b in zip(chunks, gate, up):
        o_ref[rs, :] = (a * jax.nn.sigmoid(a) * b).astype(o_ref.dtype)


def _ffn_up(x, g, wg, wu, layer, *, bm_pref=1024, bn_pref=512):
    M, K = x.shape
    N = wg.shape[2]
    bm = _pick(M, bm_pref)
    bn = _pick(N, bn_pref)
    pipelined = _nbytes((bm, K), F32) + 2 * _nbytes((K, bn), BF16) + _nbytes((bm, bn), BF16)
    resident = _nbytes((bm, K), BF16) + 4 * _nbytes((bm, bn), F32)
    return pl.pallas_call(
        _ffn_up_kernel,
        grid=(M // bm, N // bn),
        in_specs=[
            pl.BlockSpec((bm, K), lambda i, n: (i, 0)),
            pl.BlockSpec((1, K), lambda i, n: (0, 0)),
            _weight_spec(layer, K, bn),
            _weight_spec(layer, K, bn),
        ],
        out_specs=pl.BlockSpec((bm, bn), lambda i, n: (i, n)),
        out_shape=jax.ShapeDtypeStruct((M, N), BF16),
        scratch_shapes=[pltpu.VMEM((bm, K), BF16)],
        compiler_params=_compiler_params(2, pipelined, resident),
    )(x, g.reshape(1, K), wg, wu)


def _ple_kernel(x_ref, g_ref, wg_ref, p_ref, wp_ref, o_ref, xn_ref):
    n = pl.program_id(1)

    @pl.when(n == 0)
    def _():
        _rms_to_bf16(x_ref, g_ref, xn_ref)

    bn = o_ref.shape[1]
    cols = pl.ds(pl.multiple_of(n * bn, bn), bn)
    chunks = _row_chunks(xn_ref.shape[0])
    gate = [jnp.dot(xn_ref[rs, :], wg_ref[...], preferred_element_type=F32) for rs in chunks]
    emb = [jnp.dot(p_ref[rs, :].astype(BF16), wp_ref[...], preferred_element_type=F32) for rs in chunks]
    for rs, a, c in zip(chunks, gate, emb):
        o_ref[rs, :] = x_ref[rs, cols] + jax.nn.sigmoid(a) * c


def _ple(x, g, w_gate, p, w_proj, layer, *, bm_pref=1024, bn_pref=512):
    M, K = x.shape
    N = w_gate.shape[2]
    P = p.shape[1]
    bm = _pick(M, bm_pref)
    bn = _pick(N, bn_pref)
    pipelined = (_nbytes((bm, K), F32) + _nbytes((K, bn), BF16) + _nbytes((bm, P), F32)
                 + _nbytes((P, bn), BF16) + _nbytes((bm, bn), F32))
    resident = _nbytes((bm, K), BF16) + 3 * _nbytes((bm, bn), F32)
    return pl.pallas_call(
        _ple_kernel,
        grid=(M // bm, N // bn),
        in_specs=[
            pl.BlockSpec((bm, K), lambda i, n: (i, 0)),
            pl.BlockSpec((1, K), lambda i, n: (0, 0)),
            _weight_spec(layer, K, bn),
            pl.BlockSpec((bm, P), lambda i, n: (i, 0)),
            _weight_spec(layer, P, bn),
        ],
        out_specs=pl.BlockSpec((bm, bn), lambda i, n: (i, n)),
        out_shape=jax.ShapeDtypeStruct((M, N), F32),
        scratch_shapes=[pltpu.VMEM((bm, K), BF16)],
        compiler_params=_compiler_params(2, pipelined, resident),
    )(x, g.reshape(1, K), w_gate, p, w_proj)


def _log2_gates(z2):
    neg_abs = pltpu.bitcast(pltpu.bitcast(z2, jnp.uint32) | jnp.uint32(0x80000000), F32)
    sp2 = jnp.log(1.0 + jnp.exp2(neg_abs)) * LOG2_E
    log_beta = jnp.minimum(z2, 0.0) - sp2
    return log_beta, log_beta - z2


def _suffix_sum(log_keep, tri):
    return jnp.dot(log_keep.astype(BF16), tri, preferred_element_type=F32)


def _sb_attn_kernel(bias_ref, q_ref, k_ref, v_ref, o_ref, acc_ref, car_ref, *, tq, heads):
    hp = pl.program_id(1)
    i = pl.program_id(2)
    row = lax.broadcasted_iota(jnp.int32, (tq, tq), 0)
    col = lax.broadcasted_iota(jnp.int32, (tq, tq), 1)
    tri = jnp.where(row > col, 1.0, 0.0).astype(BF16)
    causal = col < row
    acc_ref[...] = jnp.zeros_like(acc_ref)
    car_ref[...] = jnp.zeros_like(car_ref)

    lanes = [slice(g * HEAD_DIM, (g + 1) * HEAD_DIM) for g in range(heads)]

    def logits(j):
        koff = pl.multiple_of(j * tq, tq)
        return [lax.dot_general(q_ref[:, sl], k_ref[pl.ds(koff, tq), sl], NT_DIMS, preferred_element_type=F32)
                for sl in lanes]

    def visit(j, zs, mask):
        koff = pl.multiple_of(j * tq, tq)
        gates = []
        for g, z in enumerate(zs):
            log_beta, log_keep = _log2_gates(z + bias_ref[hp * heads + g] * LOG2_E)
            if mask is not None:
                log_keep = jnp.where(mask, log_keep, 0.0)
            gates.append((log_beta, log_keep))
        later = [_suffix_sum(log_keep, tri) for _, log_keep in gates]
        weights = []
        for sl, (log_beta, log_keep), lt in zip(lanes, gates, later):
            carry = car_ref[:, sl]
            w = jnp.exp2(log_beta + lt + jnp.tile(carry, (1, tq // LANES)))
            if mask is not None:
                w = jnp.where(mask, w, 0.0)
            weights.append(w.astype(BF16))
            car_ref[:, sl] = carry + jnp.broadcast_to(jnp.sum(log_keep, axis=1, keepdims=True), carry.shape)
        for sl, w in zip(lanes, weights):
            acc_ref[:, sl] += jnp.dot(w, v_ref[pl.ds(koff, tq), sl], preferred_element_type=F32)

    visit(i, logits(i), causal)

    def body(jj, c):
        j = i - 1 - jj
        visit(j, logits(j), None)
        return c

    lax.fori_loop(0, i, body, 0)
    o_ref[...] = acc_ref[...].astype(o_ref.dtype)


def _sb_attention(qkv, bias, B, S, *, tq_pref=256, heads_pref=8):
    _, M, W = qkv.shape
    tq = _pick(S, tq_pref)
    heads = _pick(W // HEAD_DIM, heads_pref)
    hw = heads * HEAD_DIM
    qkv4 = qkv.reshape(3, B, S, W)

    def kv_spec(slab):
        return pl.BlockSpec((None, None, S, hw), lambda b, h, i: (slab, b, 0, h), pipeline_mode=pl.Buffered(1))

    pipelined = 2 * _nbytes((tq, hw), BF16)
    resident = 2 * _nbytes((S, hw), BF16) + 2 * _nbytes((tq, hw), F32) + 6 * heads * _nbytes((tq, tq), F32)
    out = pl.pallas_call(
        functools.partial(_sb_attn_kernel, tq=tq, heads=heads),
        grid=(B, W // hw, S // tq),
        in_specs=[
            pl.BlockSpec(memory_space=pltpu.SMEM),
            pl.BlockSpec((None, None, tq, hw), lambda b, h, i: (0, b, i, h)),
            kv_spec(1),
            kv_spec(2),
        ],
        out_specs=pl.BlockSpec((None, tq, hw), lambda b, h, i: (b, i, h)),
        out_shape=jax.ShapeDtypeStruct((B, S, W), BF16),
        scratch_shapes=[pltpu.VMEM((tq, hw), F32), pltpu.VMEM((tq, hw), F32)],
        compiler_params=_compiler_params(3, pipelined, resident),
    )(bias, qkv4, qkv4, qkv4)
    return out.reshape(M, W)


def _sb_decode_kernel(pt_ref, q_ref, bias_ref, *refs, pages, heads):
    del pt_ref
    k_refs, v_refs = refs[:pages], refs[pages:2 * pages]
    o_ref, acc_ref, car_ref = refs[2 * pages:]
    p = pl.program_id(1)
    H = heads
    rows = k_refs[0].shape[0]
    chunks = rows // LANES
    SR = chunks * H

    @pl.when(p == 0)
    def _():
        acc_ref[...] = jnp.zeros_like(acc_ref)
        car_ref[...] = jnp.zeros_like(car_ref)

    own_head = (lax.broadcasted_iota(jnp.int32, (SR, LANES), 1) % H
                == lax.broadcasted_iota(jnp.int32, (SR, LANES), 0) % H)
    src = lax.broadcasted_iota(jnp.int32, (LANES, LANES), 0)
    dst = lax.broadcasted_iota(jnp.int32, (LANES, LANES), 1)
    tri = jnp.where((src // H > dst // H) & (src % H == dst % H), 1.0, 0.0).astype(BF16)
    q = q_ref[...].astype(BF16)

    scores = [lax.dot_general(q, k_ref[...].astype(BF16), NT_DIMS, preferred_element_type=F32)
              for k_ref in k_refs]
    gates = []
    for sc in scores:
        z = jnp.concatenate([sc[:, c * LANES:(c + 1) * LANES] for c in range(chunks)], axis=0)
        log_beta, log_keep = _log2_gates(z + bias_ref[...])
        gates.append((log_beta, jnp.where(own_head, log_keep, 0.0)))
    within = [_suffix_sum(log_keep, tri) for _, log_keep in gates]
    run = car_ref[:, :1]
    weights = []
    for (log_beta, log_keep), inside in zip(gates, within):
        chunk_keep = jnp.sum(log_keep, axis=1, keepdims=True)
        after = [None] * chunks
        for c in reversed(range(chunks)):
            after[c] = run
            run = run + chunk_keep[c * H:(c + 1) * H]
        w = jnp.where(own_head, jnp.exp2(log_beta + inside + jnp.concatenate(after, axis=0)), 0.0)
        weights.append(jnp.concatenate([w[c * H:(c + 1) * H] for c in range(chunks)], axis=1).astype(BF16))
    car_ref[...] = jnp.broadcast_to(run, car_ref.shape)
    acc = acc_ref[...]
    for wm, v_ref in zip(weights, v_refs):
        acc = acc + jnp.dot(wm, v_ref[...].astype(BF16), preferred_element_type=F32)
    acc_ref[...] = acc

    @pl.when(p == pl.num_programs(1) - 1)
    def _():
        o_ref[...] = acc_ref[...]


def _sb_decode(q, cache_k, cache_v, layer, page_table, bias, *, pages_pref=4):
    DB, H, _ = q.shape
    n_layers, n_pool, page = cache_k.shape[:3]
    rows = page * H
    chunks = rows // LANES
    n_pages = page_table.shape[1]
    pages = _pick(n_pages, pages_pref)
    flat_k = cache_k.reshape(n_layers, n_pool, rows, HEAD_DIM)
    flat_v = cache_v.reshape(n_layers, n_pool, rows, HEAD_DIM)

    def page_spec(s):
        def index(b, p, pt):
            return (layer, pt[b, n_pages - 1 - (p * pages + s)], 0, 0)
        return pl.BlockSpec((None, None, rows, HEAD_DIM), index)

    pipelined = 2 * pages * _nbytes((rows, HEAD_DIM), F32)
    resident = pages * (2 * _nbytes((rows, HEAD_DIM), BF16) + 12 * _nbytes((chunks * H, LANES), F32))
    grid_spec = pltpu.PrefetchScalarGridSpec(
        num_scalar_prefetch=1,
        grid=(DB, n_pages // pages),
        in_specs=[
            pl.BlockSpec((None, H, HEAD_DIM), lambda b, p, pt: (b, 0, 0)),
            pl.BlockSpec((chunks * H, LANES), lambda b, p, pt: (0, 0)),
        ] + [page_spec(s) for s in range(pages)] * 2,
        out_specs=pl.BlockSpec((None, H, HEAD_DIM), lambda b, p, pt: (b, 0, 0)),
        scratch_shapes=[pltpu.VMEM((H, HEAD_DIM), F32), pltpu.VMEM((H, LANES), F32)],
    )
    return pl.pallas_call(
        functools.partial(_sb_decode_kernel, pages=pages, heads=H),
        grid_spec=grid_spec,
        out_shape=jax.ShapeDtypeStruct((DB, H, HEAD_DIM), F32),
        compiler_params=_compiler_params(2, pipelined, resident),
    )(page_table, q, jnp.tile(bias[:, None] * LOG2_E, (chunks, LANES)), *([flat_k] * pages), *([flat_v] * pages))


def _t5_bucket(dist):
    dist = np.asarray(dist, np.int64)
    max_exact = NUM_BUCKETS // 2
    large = max_exact + (np.log(np.maximum(dist, 1) / max_exact) / np.log(BUCKET_MAX_DIST / max_exact)
                         * (NUM_BUCKETS - max_exact)).astype(np.int64)
    large = np.minimum(large, NUM_BUCKETS - 1)
    return np.where(dist < max_exact, dist, large).astype(np.int32)


def _dil_offset_bias(rel_bias, g, heads):
    buckets = _t5_bucket(DIL_RATES[g] * np.arange(DIL_STEPS + 1))
    return rel_bias[g * heads:(g + 1) * heads][:, buckets].astype(F32)


def _dil_prompt_bias(offset_bias):
    heads = offset_bias.shape[0]
    S = DIL_STEPS
    by_shift = jnp.concatenate([jnp.full((heads, S - 1), MASK_VALUE, F32), offset_bias[:, ::-1],
                                jnp.full((heads, S), MASK_VALUE, F32)], axis=1)
    skew = jnp.tile(by_shift, (1, S))[:, :S * (3 * S - 1)].reshape(heads, S, 3 * S - 1)
    general = skew[:, :, S - 1:3 * S - 1]
    first = jnp.where(np.arange(2 * S)[None, None, :] >= S, general, MASK_VALUE)
    return jnp.stack([first, general])


def _dil_attn_kernel(q_ref, kc_ref, kp_ref, vc_ref, vp_ref, bias_ref, o_ref, l_ref, *, rate, unroll):
    not_first = jnp.minimum(pl.program_id(1), 1)
    heads = q_ref.shape[0]
    head0 = pl.program_id(2) * heads

    def body(t, c):
        tiles = []
        for u in range(unroll):
            r = t * unroll + u
            rows = pl.ds(r, DIL_STEPS, stride=rate) if rate > 1 else pl.ds(0, DIL_STEPS)
            tiles += [(h, rows) for h in range(heads)]
        scores = []
        for h, rows in tiles:
            keys = jnp.concatenate([kp_ref[h, rows, :], kc_ref[h, rows, :]], axis=0).astype(BF16)
            scores.append(lax.dot_general(q_ref[h, rows, :].astype(BF16), keys, NT_DIMS,
                                          preferred_element_type=F32))
        probs = []
        for (h, rows), s in zip(tiles, scores):
            s = s * ATTN_SCALE + bias_ref[not_first, head0 + h]
            m = jnp.max(s, axis=1, keepdims=True)
            p = jnp.exp(s - m)
            l = jnp.sum(p, axis=1, keepdims=True)
            l_ref[h, rows, :] = jnp.broadcast_to(m + jnp.log(l), (DIL_STEPS, HEAD_DIM))
            probs.append((p.astype(BF16), l))
        for (h, rows), (p, l) in zip(tiles, probs):
            vals = jnp.concatenate([vp_ref[h, rows, :], vc_ref[h, rows, :]], axis=0).astype(BF16)
            o_ref[h, rows, :] = jnp.dot(p, vals, preferred_element_type=F32) / l
        return c

    lax.fori_loop(0, rate // unroll, body, 0)


def _dil_group_attention(qkv, g, bias, B, S, *, block_bytes=2 * 2**20):
    n_slabs, M, _ = qkv.shape
    H = n_slabs // (3 * N_DIL_GROUPS)
    d = DIL_RATES[g]
    chunk = DIL_STEPS * d
    assert S % chunk == 0, "sequence must be a whole number of dilation chunks"
    nc = S // chunk
    hb = _pick(H, max(block_bytes // _nbytes((chunk, HEAD_DIM), F32), 1))
    blk = (hb, chunk, HEAD_DIM)

    def spec(slab, prev):
        base = slab * (H // hb)
        if prev:
            return pl.BlockSpec(blk, lambda b, c, h: (base + h, b * nc + jnp.maximum(c - 1, 0), 0))
        return pl.BlockSpec(blk, lambda b, c, h: (base + h, b * nc + c, 0))

    out_spec = pl.BlockSpec(blk, lambda b, c, h: (h, b * nc + c, 0))
    out_sds = jax.ShapeDtypeStruct((H, M, HEAD_DIM), F32)
    unroll = max(1, min(d, 8 // hb))
    pipelined = 7 * _nbytes(blk, F32)
    resident = 2 * _nbytes(bias.shape, F32) + 16 * unroll * hb * _nbytes((DIL_STEPS, 2 * DIL_STEPS), F32)
    return pl.pallas_call(
        functools.partial(_dil_attn_kernel, rate=d, unroll=unroll),
        grid=(B, nc, H // hb),
        in_specs=[
            spec(3 * g, False), spec(3 * g + 1, False), spec(3 * g + 1, True),
            spec(3 * g + 2, False), spec(3 * g + 2, True),
            pl.BlockSpec(bias.shape, lambda b, c, h: (0, 0, 0, 0)),
        ],
        out_specs=[out_spec, out_spec],
        out_shape=[out_sds, out_sds],
        compiler_params=_compiler_params(3, pipelined, resident),
    )(qkv, qkv, qkv, qkv, qkv, bias)


def _merge_groups(outs, lses):
    m = functools.reduce(jnp.maximum, lses)
    es = [jnp.exp(l - m) for l in lses]
    den = functools.reduce(jnp.add, es)
    num = functools.reduce(jnp.add, [e * o for e, o in zip(es, outs)])
    return num / den


def _dil_merge_outproj_kernel(*refs):
    G = N_DIL_GROUPS
    o_refs, l_refs = refs[:G], refs[G:2 * G]
    w_ref, r_ref, out_ref, mg_ref = refs[2 * G:]

    @pl.when(pl.program_id(1) == 0)
    def _():
        for h in range(o_refs[0].shape[0]):
            merged = _merge_groups([o[h] for o in o_refs], [l[h] for l in l_refs])
            mg_ref[:, h * HEAD_DIM:(h + 1) * HEAD_DIM] = merged.astype(BF16)

    out_ref[...] = r_ref[...] + jnp.dot(mg_ref[...], w_ref[...], preferred_element_type=F32)


def _dil_merge_outproj(outs, lses, w, layer, res, *, bm_pref=512, bn_pref=1024):
    H, M, _ = outs[0].shape
    K = H * HEAD_DIM
    N = w.shape[2]
    bm = _pick(M, bm_pref)
    bn = _pick(N, bn_pref)
    row_spec = pl.BlockSpec((H, bm, HEAD_DIM), lambda i, n: (0, i, 0))
    pipelined = 2 * N_DIL_GROUPS * _nbytes((bm, K), F32) + _nbytes((K, bn), BF16) + 2 * _nbytes((bm, bn), F32)
    resident = _nbytes((bm, K), BF16) + 6 * _nbytes((bm, K), F32)
    return pl.pallas_call(
        _dil_merge_outproj_kernel,
        grid=(M // bm, N // bn),
        in_specs=[row_spec] * (2 * N_DIL_GROUPS) + [
            _weight_spec(layer, K, bn),
            pl.BlockSpec((bm, bn), lambda i, n: (i, n)),
        ],
        out_specs=pl.BlockSpec((bm, bn), lambda i, n: (i, n)),
        out_shape=jax.ShapeDtypeStruct((M, N), F32),
        scratch_shapes=[pltpu.VMEM((bm, K), BF16)],
        compiler_params=_compiler_params(2, pipelined, resident),
    )(*outs, *lses, w, res)


def _dil_decode_kernel(qkv_ref, w0_ref, w1_ref, w2_ref, wb_ref, sb_ref, o_ref):
    Hd = o_ref.shape[0]
    outs, lses = [], []
    for g, win_ref in enumerate((w0_ref, w1_ref, w2_ref)):
        flat = win_ref[...].reshape(DIL_STEPS * 2 * Hd, HEAD_DIM).astype(BF16)
        q = qkv_ref[3 * g]
        k_new = qkv_ref[3 * g + 1]
        v_new = qkv_ref[3 * g + 2]
        s_win = lax.dot_general(q.astype(BF16), flat, NT_DIMS, preferred_element_type=F32) * ATTN_SCALE
        s_win = s_win + wb_ref[g]
        s_new = jnp.sum(q * k_new, axis=1, keepdims=True) * ATTN_SCALE + sb_ref[g][:, :1]
        m = jnp.maximum(jnp.max(s_win, axis=1, keepdims=True), s_new)
        p_win = jnp.exp(s_win - m)
        p_new = jnp.exp(s_new - m)
        l = jnp.sum(p_win, axis=1, keepdims=True) + p_new
        p_on_v = pltpu.roll(p_win, Hd, axis=1)
        pv = jnp.dot(p_on_v.astype(BF16), flat, preferred_element_type=F32) + p_new * v_new
        outs.append(pv / l)
        lses.append(m + jnp.log(l))
    o_ref[...] = _merge_groups(outs, lses)


def _dil_decode(qkv, states, layer, win_bias, self_bias):
    DB, n_slabs, Hd, _ = qkv.shape
    views = []
    for g, st in enumerate(states):
        L = st.shape[2]
        assert L == DIL_WINDOWS[g], "window state must hold exactly one window"
        d = DIL_RATES[g]
        views.append(st.reshape(st.shape[0], DB, L // d, d * 2 * Hd, HEAD_DIM))
    win_spec = pl.BlockSpec((None, None, DIL_STEPS, 2 * Hd, HEAD_DIM), lambda b: (layer, b, 0, 0, 0))
    pipelined = 3 * _nbytes((DIL_STEPS, 2 * Hd, HEAD_DIM), F32) + _nbytes((n_slabs, Hd, HEAD_DIM), F32)
    resident = (3 * _nbytes((DIL_STEPS, 2 * Hd, HEAD_DIM), F32) + 2 * _nbytes(win_bias.shape, F32)
                + 24 * _nbytes((Hd, DIL_STEPS * 2 * Hd), F32))
    return pl.pallas_call(
        _dil_decode_kernel,
        grid=(DB,),
        in_specs=[
            pl.BlockSpec((None, n_slabs, Hd, HEAD_DIM), lambda b: (b, 0, 0, 0)),
            win_spec, win_spec, win_spec,
            pl.BlockSpec(win_bias.shape, lambda b: (0, 0, 0)),
            pl.BlockSpec(self_bias.shape, lambda b: (0, 0, 0)),
        ],
        out_specs=pl.BlockSpec((None, Hd, HEAD_DIM), lambda b: (b, 0, 0)),
        out_shape=jax.ShapeDtypeStruct((DB, Hd, HEAD_DIM), F32),
        compiler_params=_compiler_params(1, pipelined, resident),
    )(qkv, *views, win_bias, self_bias)


def _dil_decode_bias(offset_bias):
    heads = offset_bias.shape[0]
    own_k = np.arange(2 * heads)[None, :] == np.arange(heads)[:, None]
    table = jnp.where(own_k[:, None, :], offset_bias[:, :0:-1, None], MASK_VALUE)
    return table.reshape(heads, -1), jnp.broadcast_to(offset_bias[:, :1], (heads, LANES))


def _window_shift_kernel(cur_ref, nxt_ref, new_ref, o_ref):
    R = o_ref.shape[0]
    last = pl.program_id(1) == pl.num_programs(1) - 1
    if R > 1:
        o_ref[0:R - 1] = cur_ref[1:R]
    o_ref[R - 1] = jnp.where(last, new_ref[0], nxt_ref[0])


def _window_shift(state, new_rows, *, block_bytes=2 * 2**20):
    n_layers, DB, L, two, Hd, _ = state.shape
    row = (two * Hd, HEAD_DIM)
    R = _pick(L, max(block_bytes // _nbytes(row, F32), 1))
    flat = state.reshape(n_layers * DB, L, *row)
    out = pl.pallas_call(
        _window_shift_kernel,
        grid=(n_layers * DB, L // R),
        in_specs=[
            pl.BlockSpec((None, R, *row), lambda s, c: (s, c, 0, 0)),
            pl.BlockSpec((None, 1, *row), lambda s, c: (s, jnp.minimum((c + 1) * R, L - 1), 0, 0)),
            pl.BlockSpec((None, 1, *row), lambda s, c: (s, 0, 0, 0)),
        ],
        out_specs=pl.BlockSpec((None, R, *row), lambda s, c: (s, c, 0, 0)),
        out_shape=jax.ShapeDtypeStruct(flat.shape, state.dtype),
        compiler_params=_compiler_params(2, 2 * _nbytes((R, *row), F32), 0),
    )(flat, flat, new_rows.reshape(n_layers * DB, 1, *row))
    return out.reshape(state.shape)


def _pad_rows(x, rows):
    return jnp.pad(x, ((0, rows - x.shape[0]), (0, 0)))


def kernel(x_prompt, x_sample, cache_sb_k, cache_sb_v, state_win0_kv, state_win1_kv, state_win2_kv, page_table, p_prompt, p_sample, g_mix, g_ffn, g_ple, w_qkv_sb, w_o_sb, b_sb, w_qkv_dil, w_o_dil, g_qnorm_dil, g_knorm_dil, rel_bias, w_ffn_gate, w_ffn_up, w_ffn_down, w_ple_proj, w_ple_gate):
    B, S, D = x_prompt.shape
    DB, T, _ = x_sample.shape
    assert T == 1, "decode step handles one new token per sample row"
    depth = g_mix.shape[0]
    M = B * S
    sb_heads = D // HEAD_DIM
    dil_w = w_o_dil.shape[1]
    dil_heads = dil_w // HEAD_DIM
    win_states = (state_win0_kv, state_win1_kv, state_win2_kv)

    hp = x_prompt.reshape(M, D)
    hs = _pad_rows(x_sample.reshape(DB, D), SAMPLE_ROWS)

    w_qkv_sb, w_o_sb, w_qkv_dil, w_o_dil, w_ffn_gate, w_ffn_up, w_ffn_down, w_ple_proj, w_ple_gate = [
        _cast_bf16(w) for w in (w_qkv_sb, w_o_sb, w_qkv_dil, w_o_dil, w_ffn_gate, w_ffn_up, w_ffn_down,
                                w_ple_proj, w_ple_gate)]

    offset_bias = [_dil_offset_bias(rel_bias, g, dil_heads) for g in range(N_DIL_GROUPS)]
    prompt_bias = [_dil_prompt_bias(ob) for ob in offset_bias]
    win_bias, self_bias = (jnp.stack(t) for t in zip(*[_dil_decode_bias(ob) for ob in offset_bias]))

    n_sb = w_qkv_sb.shape[0]
    sb_prompt_kv = [jnp.zeros((n_sb, M, sb_heads, HEAD_DIM), F32) for _ in range(2)]
    sb_sample_kv = [jnp.zeros((n_sb, SAMPLE_ROWS, sb_heads, HEAD_DIM), F32) for _ in range(2)]
    win_p = [[] for _ in range(N_DIL_GROUPS)]
    win_s = [[] for _ in range(N_DIL_GROUPS)]

    for i in range(depth):
        j = i // 2
        if i % 2 == 0:
            k_p, v_p, qkv = _sb_qkv(hp, g_mix[i], w_qkv_sb, j, sb_prompt_kv)
            sb_prompt_kv = (k_p, v_p)
            attn = _sb_attention(qkv, b_sb[j], B, S)
            hp = _matmul_res(attn, w_o_sb, j, hp)

            k_s, v_s, qkv_s = _sb_qkv(hs, g_mix[i], w_qkv_sb, j, sb_sample_kv)
            sb_sample_kv = (k_s, v_s)
            q_s = qkv_s[0, :DB].astype(F32).reshape(DB, sb_heads, HEAD_DIM)
            attn_s = _sb_decode(q_s, cache_sb_k, cache_sb_v, j, page_table, b_sb[j])
            hs = _matmul_res(_pad_rows(attn_s.reshape(DB, D), SAMPLE_ROWS), w_o_sb, j, hs)
        else:
            ones = jnp.ones((HEAD_DIM,), F32)
            head_gain = jnp.stack([jnp.tile(gain, dil_heads)
                                   for g in range(N_DIL_GROUPS)
                                   for gain in (g_qnorm_dil[j, g], g_knorm_dil[j, g], ones)])[:, None, :]
            qkv = _dil_qkv(hp, g_mix[i], w_qkv_dil, j, head_gain)
            outs, lses = zip(*[_dil_group_attention(qkv, g, prompt_bias[g], B, S)
                               for g in range(N_DIL_GROUPS)])
            hp = _dil_merge_outproj(outs, lses, w_o_dil, j, hp)
            kv6 = qkv.reshape(N_DIL_GROUPS, 3, dil_heads, B, S, HEAD_DIM)
            for g in range(N_DIL_GROUPS):
                L = min(DIL_WINDOWS[g], S)
                win_p[g].append(jnp.transpose(kv6[g, 1:3, :, :, S - L:], (2, 3, 0, 1, 4)))

            qkv_s = _dil_qkv(hs, g_mix[i], w_qkv_dil, j, head_gain)[:, :DB]
            qkv_s = jnp.transpose(qkv_s.reshape(3 * N_DIL_GROUPS, dil_heads, DB, HEAD_DIM), (2, 0, 1, 3))
            attn_s = _dil_decode(qkv_s, win_states, j, win_bias, self_bias)
            hs = _matmul_res(_pad_rows(attn_s.reshape(DB, dil_w), SAMPLE_ROWS), w_o_dil, j, hs)
            for g in range(N_DIL_GROUPS):
                win_s[g].append(qkv_s[:, None, 3 * g + 1:3 * g + 3])

        hp = _matmul_res(_ffn_up(hp, g_ffn[i], w_ffn_gate, w_ffn_up, i), w_ffn_down, i, hp, bm_pref=1024)
        hp = _ple(hp, g_ple[i], w_ple_gate, p_prompt[i].reshape(M, -1), w_ple_proj, i)
        hs = _matmul_res(_ffn_up(hs, g_ffn[i], w_ffn_gate, w_ffn_up, i), w_ffn_down, i, hs)
        hs = _ple(hs, g_ple[i], w_ple_gate, _pad_rows(p_sample[i].reshape(DB, -1), SAMPLE_ROWS), w_ple_proj, i)

    new_states = [_window_shift(st, jnp.stack(rows)) for st, rows in zip(win_states, win_s)]
    return (hp.reshape(B, S, D), hs[:DB].reshape(DB, 1, D),
            *[t.reshape(-1, B, S, sb_heads, HEAD_DIM) for t in sb_prompt_kv],
            *[t[:, :DB].reshape(-1, DB, 1, sb_heads, HEAD_DIM) for t in sb_sample_kv],
            jnp.stack(win_p[0]), jnp.stack(win_p[1]), jnp.stack(win_p[2]), *new_states)
```

```python
import functools

import numpy as np
import jax
import jax.numpy as jnp
from jax import lax
from jax.experimental import pallas as pl
from jax.experimental.pallas import tpu as pltpu

F32 = jnp.float32
BF16 = jnp.bfloat16

HEAD_DIM = 128
LANES = 128
RMS_EPS = 1e-6
ATTN_SCALE = HEAD_DIM ** -0.5
LOG2_E = float(np.log2(np.e))
DIL_WINDOWS = (128, 512, 2048)
DIL_RATES = (1, 4, 16)
DIL_STEPS = 128
N_DIL_GROUPS = 3
NUM_BUCKETS = 32
BUCKET_MAX_DIST = 2048
MASK_VALUE = -1e30
SAMPLE_ROWS = 16

V7X_VMEM_BYTES = 64 * 2**20
VMEM_CEILING = V7X_VMEM_BYTES - 8 * 2**20

NT_DIMS = (((1,), (1,)), ((), ()))


def _compiler_params(n_grid, pipelined_bytes, resident_bytes):
    need = 2 * pipelined_bytes + resident_bytes + 4 * 2**20
    return pltpu.CompilerParams(
        dimension_semantics=("arbitrary",) * n_grid,
        vmem_limit_bytes=int(min(max(need, 16 * 2**20), VMEM_CEILING)))


def _nbytes(shape, dtype):
    return int(np.prod(shape)) * jnp.dtype(dtype).itemsize


def _pick(total, preferred):
    b = min(total, preferred)
    while total % b:
        b //= 2
    return b


def _cast_kernel(w_ref, o_ref):
    o_ref[...] = w_ref[...].astype(o_ref.dtype)


def _cast_bf16(w, *, bn_pref=2048, block_bytes=4 * 2**20):
    L, K, N = w.shape
    bn = _pick(N, bn_pref)
    bk = _pick(K, block_bytes // (4 * bn))
    spec = pl.BlockSpec((None, bk, bn), lambda l, k, n: (l, k, n))
    return pl.pallas_call(
        _cast_kernel,
        grid=(L, K // bk, N // bn),
        in_specs=[spec],
        out_specs=spec,
        out_shape=jax.ShapeDtypeStruct(w.shape, BF16),
        compiler_params=_compiler_params(3, _nbytes((bk, bn), F32) + _nbytes((bk, bn), BF16), 0),
    )(w)


def _weight_spec(layer, K, bn):
    return pl.BlockSpec((None, K, bn), lambda i, n: (layer, 0, n))


def _row_chunks(rows, preferred=256):
    c = _pick(rows, preferred)
    return [slice(r, r + c) for r in range(0, rows, c)]


def _rms_to_bf16(x_ref, g_ref, xn_ref):
    x = x_ref[...]
    ms = jnp.mean(x * x, axis=-1, keepdims=True)
    xn_ref[...] = (x * lax.rsqrt(ms + RMS_EPS) * g_ref[...]).astype(BF16)


def _qkv_kernel(x_ref, g_ref, w_ref, k_in_ref, v_in_ref, k_ref, v_ref, qkv_ref, xn_ref, *, per):
    del k_in_ref, v_in_ref
    n = pl.program_id(1)

    @pl.when(n == 0)
    def _():
        _rms_to_bf16(x_ref, g_ref, xn_ref)

    def slab(store):
        chunks = _row_chunks(xn_ref.shape[0])
        ys = [jnp.dot(xn_ref[rs, :], w_ref[...], preferred_element_type=F32) for rs in chunks]
        for rs, y in zip(chunks, ys):
            store(rs, y)

    @pl.when(n < per)
    def _():
        def store(rs, y):
            qkv_ref[rs, :] = (y * (ATTN_SCALE * LOG2_E)).astype(BF16)
        slab(store)

    for first, kv_ref in ((per, k_ref), (2 * per, v_ref)):
        @pl.when(jnp.logical_and(n >= first, n < first + per))
        def _():
            def store(rs, y):
                qkv_ref[rs, :] = y.astype(BF16)
                kv_ref[rs] = y.reshape((y.shape[0],) + kv_ref.shape[1:])
            slab(store)


def _sb_qkv(x, g, w, layer, kv, *, bm_pref=1024, bn_pref=1024):
    M, K = x.shape
    L, _, N = w.shape
    D = N // 3
    H = D // HEAD_DIM
    bm = _pick(M, bm_pref)
    bn = _pick(D, bn_pref)
    hb = bn // HEAD_DIM
    per = D // bn

    def kv_spec(slab):
        return pl.BlockSpec((None, bm, hb, HEAD_DIM),
                            lambda i, n: (layer, i, jnp.clip(n - slab * per, 0, per - 1), 0))

    kv_sds = jax.ShapeDtypeStruct((L, M, H, HEAD_DIM), F32)
    pipelined = _nbytes((bm, K), F32) + _nbytes((K, bn), BF16) + 2 * _nbytes((bm, bn), F32) + _nbytes((bm, bn), BF16)
    resident = _nbytes((bm, K), BF16) + 2 * _nbytes((bm, bn), F32)
    return pl.pallas_call(
        functools.partial(_qkv_kernel, per=per),
        grid=(M // bm, N // bn),
        in_specs=[
            pl.BlockSpec((bm, K), lambda i, n: (i, 0)),
            pl.BlockSpec((1, K), lambda i, n: (0, 0)),
            _weight_spec(layer, K, bn),
        ] + [pl.BlockSpec(memory_space=pl.ANY)] * 2,
        out_specs=[kv_spec(1), kv_spec(2), pl.BlockSpec((None, bm, bn), lambda i, n: (n // per, i, n % per))],
        out_shape=[kv_sds, kv_sds, jax.ShapeDtypeStruct((3, M, D), BF16)],
        input_output_aliases={3: 0, 4: 1},
        scratch_shapes=[pltpu.VMEM((bm, K), BF16)],
        compiler_params=_compiler_params(2, pipelined, resident),
    )(x, g.reshape(1, K), w, *kv)


def _dil_qkv_kernel(x_ref, g_ref, w_ref, hg_ref, o_ref, xn_ref):
    n = pl.program_id(1)

    @pl.when(n == 0)
    def _():
        _rms_to_bf16(x_ref, g_ref, xn_ref)

    is_v = (n % 3) == 2
    chunks = _row_chunks(xn_ref.shape[0])
    ys = [jnp.dot(xn_ref[rs, :], w_ref[...], preferred_element_type=F32) for rs in chunks]
    for rs, y in zip(chunks, ys):
        for h in range(o_ref.shape[0]):
            sl = slice(h * HEAD_DIM, (h + 1) * HEAD_DIM)
            yh = y[:, sl]
            ms = jnp.mean(yh * yh, axis=-1, keepdims=True)
            o_ref[h, rs, :] = jnp.where(is_v, yh, yh * lax.rsqrt(ms + RMS_EPS) * hg_ref[:, sl])


def _dil_qkv(x, g, w, layer, head_gain, *, bm_pref=1024):
    M, K = x.shape
    N = w.shape[2]
    W = head_gain.shape[2]
    heads = W // HEAD_DIM
    bm = _pick(M, bm_pref)
    pipelined = _nbytes((bm, K), F32) + _nbytes((K, W), BF16) + _nbytes((bm, W), F32)
    resident = _nbytes((bm, K), BF16) + 2 * _nbytes((bm, W), F32)
    return pl.pallas_call(
        _dil_qkv_kernel,
        grid=(M // bm, N // W),
        in_specs=[
            pl.BlockSpec((bm, K), lambda i, n: (i, 0)),
            pl.BlockSpec((1, K), lambda i, n: (0, 0)),
            _weight_spec(layer, K, W),
            pl.BlockSpec((None, 1, W), lambda i, n: (n, 0, 0)),
        ],
        out_specs=pl.BlockSpec((heads, bm, HEAD_DIM), lambda i, n: (n, i, 0)),
        out_shape=jax.ShapeDtypeStruct((N // HEAD_DIM, M, HEAD_DIM), F32),
        scratch_shapes=[pltpu.VMEM((bm, K), BF16)],
        compiler_params=_compiler_params(2, pipelined, resident),
    )(x, g.reshape(1, K), w, head_gain)


def _matmul_res_kernel(x_ref, w_ref, r_ref, o_ref):
    chunks = _row_chunks(x_ref.shape[0])
    ys = [jnp.dot(x_ref[rs, :].astype(BF16), w_ref[...], preferred_element_type=F32) for rs in chunks]
    for rs, y in zip(chunks, ys):
        o_ref[rs, :] = r_ref[rs, :] + y


def _matmul_res(x, w, layer, res, *, bm_pref=512, bn_pref=512):
    M, K = x.shape
    N = w.shape[2]
    bm = _pick(M, bm_pref)
    bn = _pick(N, bn_pref)
    pipelined = _nbytes((bm, K), x.dtype) + _nbytes((K, bn), BF16) + 2 * _nbytes((bm, bn), F32)
    resident = (0 if x.dtype == BF16 else _nbytes((bm, K), BF16)) + _nbytes((bm, bn), F32)
    return pl.pallas_call(
        _matmul_res_kernel,
        grid=(M // bm, N // bn),
        in_specs=[
            pl.BlockSpec((bm, K), lambda i, n: (i, 0)),
            _weight_spec(layer, K, bn),
            pl.BlockSpec((bm, bn), lambda i, n: (i, n)),
        ],
        out_specs=pl.BlockSpec((bm, bn), lambda i, n: (i, n)),
        out_shape=jax.ShapeDtypeStruct((M, N), F32),
        compiler_params=_compiler_params(2, pipelined, resident),
    )(x, w, res)


def _ffn_up_kernel(x_ref, g_ref, wg_ref, wu_ref, o_ref, xn_ref):
    @pl.when(pl.program_id(1) == 0)
    def _():
        _rms_to_bf16(x_ref, g_ref, xn_ref)

    chunks = _row_chunks(xn_ref.shape[0])
    gate = [jnp.dot(xn_ref[rs, :], wg_ref[...], preferred_element_type=F32) for rs in chunks]
    up = [jnp.dot(xn_ref[rs, :], wu_ref[...], preferred_element_type=F32) for rs in chunks]
    for rs, a, b in zip(chunks, gate, up):
        o_ref[rs, :] = (a * jax.nn.sigmoid(a) * b).astype(o_ref.dtype)


def _ffn_up(x, g, wg, wu, layer, *, bm_pref=1024, bn_pref=512):
    M, K = x.shape
    N = wg.shape[2]
    bm = _pick(M, bm_pref)
    bn = _pick(N, bn_pref)
    pipelined = _nbytes((bm, K), F32) + 2 * _nbytes((K, bn), BF16) + _nbytes((bm, bn), BF16)
    resident = _nbytes((bm, K), BF16) + 4 * _nbytes((bm, bn), F32)
    return pl.pallas_call(
        _ffn_up_kernel,
        grid=(M // bm, N // bn),
        in_specs=[
            pl.BlockSpec((bm, K), lambda i, n: (i, 0)),
            pl.BlockSpec((1, K), lambda i, n: (0, 0)),
            _weight_spec(layer, K, bn),
            _weight_spec(layer, K, bn),
        ],
        out_specs=pl.BlockSpec((bm, bn), lambda i, n: (i, n)),
        out_shape=jax.ShapeDtypeStruct((M, N), BF16),
        scratch_shapes=[pltpu.VMEM((bm, K), BF16)],
        compiler_params=_compiler_params(2, pipelined, resident),
    )(x, g.reshape(1, K), wg, wu)


def _ple_kernel(x_ref, g_ref, wg_ref, p_ref, wp_ref, o_ref, xn_ref):
    n = pl.program_id(1)

    @pl.when(n == 0)
    def _():
        _rms_to_bf16(x_ref, g_ref, xn_ref)

    bn = o_ref.shape[1]
    cols = pl.ds(pl.multiple_of(n * bn, bn), bn)
    chunks = _row_chunks(xn_ref.shape[0])
    gate = [jnp.dot(xn_ref[rs, :], wg_ref[...], preferred_element_type=F32) for rs in chunks]
    emb = [jnp.dot(p_ref[rs, :].astype(BF16), wp_ref[...], preferred_element_type=F32) for rs in chunks]
    for rs, a, c in zip(chunks, gate, emb):
        o_ref[rs, :] = x_ref[rs, cols] + jax.nn.sigmoid(a) * c


def _ple(x, g, w_gate, p, w_proj, layer, *, bm_pref=1024, bn_pref=1024):
    M, K = x.shape
    N = w_gate.shape[2]
    P = p.shape[1]
    bm = _pick(M, bm_pref)
    bn = _pick(N, bn_pref)
    pipelined = (_nbytes((bm, K), F32) + _nbytes((K, bn), BF16) + _nbytes((bm, P), F32)
                 + _nbytes((P, bn), BF16) + _nbytes((bm, bn), F32))
    resident = _nbytes((bm, K), BF16) + 3 * _nbytes((bm, bn), F32)
    return pl.pallas_call(
        _ple_kernel,
        grid=(M // bm, N // bn),
        in_specs=[
            pl.BlockSpec((bm, K), lambda i, n: (i, 0)),
            pl.BlockSpec((1, K), lambda i, n: (0, 0)),
            _weight_spec(layer, K, bn),
            pl.BlockSpec((bm, P), lambda i, n: (i, 0)),
            _weight_spec(layer, P, bn),
        ],
        out_specs=pl.BlockSpec((bm, bn), lambda i, n: (i, n)),
        out_shape=jax.ShapeDtypeStruct((M, N), F32),
        scratch_shapes=[pltpu.VMEM((bm, K), BF16)],
        compiler_params=_compiler_params(2, pipelined, resident),
    )(x, g.reshape(1, K), w_gate, p, w_proj)


def _log2_gates(z2):
    neg_abs = pltpu.bitcast(pltpu.bitcast(z2, jnp.uint32) | jnp.uint32(0x80000000), F32)
    sp2 = jnp.log(1.0 + jnp.exp2(neg_abs)) * LOG2_E
    log_beta = jnp.minimum(z2, 0.0) - sp2
    return log_beta, log_beta - z2


def _suffix_sum(log_keep, tri):
    return jnp.dot(log_keep.astype(BF16), tri, preferred_element_type=F32)


def _sb_attn_kernel(bias_ref, q_ref, k_ref, v_ref, o_ref, acc_ref, car_ref, *, tq, heads):
    hp = pl.program_id(1)
    i = pl.program_id(2)
    row = lax.broadcasted_iota(jnp.int32, (tq, tq), 0)
    col = lax.broadcasted_iota(jnp.int32, (tq, tq), 1)
    tri = jnp.where(row > col, 1.0, 0.0).astype(BF16)
    causal = col < row
    acc_ref[...] = jnp.zeros_like(acc_ref)
    car_ref[...] = jnp.zeros_like(car_ref)

    lanes = [slice(g * HEAD_DIM, (g + 1) * HEAD_DIM) for g in range(heads)]

    def logits(j):
        rows = pl.ds(pl.multiple_of(j * tq, tq), tq)
        return [lax.dot_general(q_ref[:, sl], k_ref[rows, sl], NT_DIMS, preferred_element_type=F32)
                for sl in lanes]

    def visit(j, zs, mask):
        rows = pl.ds(pl.multiple_of(j * tq, tq), tq)
        gates = []
        for g, z in enumerate(zs):
            log_beta, log_keep = _log2_gates(z + bias_ref[hp * heads + g] * LOG2_E)
            if mask is not None:
                log_keep = jnp.where(mask, log_keep, 0.0)
            gates.append((log_beta, log_keep))
        later = [_suffix_sum(log_keep, tri) for _, log_keep in gates]
        weights = []
        for sl, (log_beta, log_keep), lt in zip(lanes, gates, later):
            carry = car_ref[:, sl]
            w = jnp.exp2(log_beta + lt + jnp.tile(carry, (1, tq // LANES)))
            if mask is not None:
                w = jnp.where(mask, w, 0.0)
            weights.append(w.astype(BF16))
            car_ref[:, sl] = carry + jnp.broadcast_to(jnp.sum(log_keep, axis=1, keepdims=True), carry.shape)
        for sl, w in zip(lanes, weights):
            acc_ref[:, sl] += jnp.dot(w, v_ref[rows, sl], preferred_element_type=F32)

    visit(i, logits(i), causal)

    def body(jj, c):
        j = i - 1 - jj
        visit(j, logits(j), None)
        return c

    lax.fori_loop(0, i, body, 0)
    o_ref[...] = acc_ref[...].astype(o_ref.dtype)


def _sb_attention(qkv, bias, B, S, *, tq_pref=256, heads_pref=8):
    _, M, W = qkv.shape
    tq = _pick(S, tq_pref)
    heads = _pick(W // HEAD_DIM, heads_pref)
    hw = heads * HEAD_DIM
    qkv4 = qkv.reshape(3, B, S, W)

    def kv_spec(slab):
        return pl.BlockSpec((None, None, S, hw), lambda b, h, i: (slab, b, 0, h), pipeline_mode=pl.Buffered(1))

    pipelined = 2 * _nbytes((tq, hw), BF16)
    resident = 2 * _nbytes((S, hw), BF16) + 2 * _nbytes((tq, hw), F32) + 6 * heads * _nbytes((tq, tq), F32)
    out = pl.pallas_call(
        functools.partial(_sb_attn_kernel, tq=tq, heads=heads),
        grid=(B, W // hw, S // tq),
        in_specs=[
            pl.BlockSpec(memory_space=pltpu.SMEM),
            pl.BlockSpec((None, None, tq, hw), lambda b, h, i: (0, b, i, h)),
            kv_spec(1),
            kv_spec(2),
        ],
        out_specs=pl.BlockSpec((None, tq, hw), lambda b, h, i: (b, i, h)),
        out_shape=jax.ShapeDtypeStruct((B, S, W), BF16),
        scratch_shapes=[pltpu.VMEM((tq, hw), F32), pltpu.VMEM((tq, hw), F32)],
        compiler_params=_compiler_params(3, pipelined, resident),
    )(bias, qkv4, qkv4, qkv4)
    return out.reshape(M, W)


def _sb_decode_kernel(pt_ref, q_ref, bias_ref, *refs, pages, heads):
    del pt_ref
    k_refs, v_refs = refs[:pages], refs[pages:2 * pages]
    o_ref, acc_ref, car_ref = refs[2 * pages:]
    p = pl.program_id(1)
    H = heads
    rows = k_refs[0].shape[0]
    chunks = rows // LANES
    SR = chunks * H

    @pl.when(p == 0)
    def _():
        acc_ref[...] = jnp.zeros_like(acc_ref)
        car_ref[...] = jnp.zeros_like(car_ref)

    own_head = (lax.broadcasted_iota(jnp.int32, (SR, LANES), 1) % H
                == lax.broadcasted_iota(jnp.int32, (SR, LANES), 0) % H)
    src = lax.broadcasted_iota(jnp.int32, (LANES, LANES), 0)
    dst = lax.broadcasted_iota(jnp.int32, (LANES, LANES), 1)
    tri = jnp.where((src // H > dst // H) & (src % H == dst % H), 1.0, 0.0).astype(BF16)
    q = q_ref[...].astype(BF16)

    scores = [lax.dot_general(q, k_ref[...].astype(BF16), NT_DIMS, preferred_element_type=F32)
              for k_ref in k_refs]
    gates = []
    for sc in scores:
        z = jnp.concatenate([sc[:, c * LANES:(c + 1) * LANES] for c in range(chunks)], axis=0)
        log_beta, log_keep = _log2_gates(z + bias_ref[...])
        gates.append((log_beta, jnp.where(own_head, log_keep, 0.0)))
    within = [_suffix_sum(log_keep, tri) for _, log_keep in gates]
    run = car_ref[:, :1]
    weights = []
    for (log_beta, log_keep), inside in zip(gates, within):
        chunk_keep = jnp.sum(log_keep, axis=1, keepdims=True)
        after = [None] * chunks
        for c in reversed(range(chunks)):
            after[c] = run
            run = run + chunk_keep[c * H:(c + 1) * H]
        w = jnp.where(own_head, jnp.exp2(log_beta + inside + jnp.concatenate(after, axis=0)), 0.0)
        weights.append(jnp.concatenate([w[c * H:(c + 1) * H] for c in range(chunks)], axis=1).astype(BF16))
    car_ref[...] = jnp.broadcast_to(run, car_ref.shape)
    acc = acc_ref[...]
    for wm, v_ref in zip(weights, v_refs):
        acc = acc + jnp.dot(wm, v_ref[...].astype(BF16), preferred_element_type=F32)
    acc_ref[...] = acc

    @pl.when(p == pl.num_programs(1) - 1)
    def _():
        o_ref[...] = acc_ref[...]


def _sb_decode(q, cache_k, cache_v, layer, page_table, bias, *, pages_pref=4):
    DB, H, _ = q.shape
    n_layers, n_pool, page = cache_k.shape[:3]
    rows = page * H
    chunks = rows // LANES
    n_pages = page_table.shape[1]
    pages = _pick(n_pages, pages_pref)
    flat_k = cache_k.reshape(n_layers, n_pool, rows, HEAD_DIM)
    flat_v = cache_v.reshape(n_layers, n_pool, rows, HEAD_DIM)

    def page_spec(s):
        def index(b, p, pt):
            return (layer, pt[b, n_pages - 1 - (p * pages + s)], 0, 0)
        return pl.BlockSpec((None, None, rows, HEAD_DIM), index)

    pipelined = 2 * pages * _nbytes((rows, HEAD_DIM), F32)
    resident = pages * (2 * _nbytes((rows, HEAD_DIM), BF16) + 12 * _nbytes((chunks * H, LANES), F32))
    grid_spec = pltpu.PrefetchScalarGridSpec(
        num_scalar_prefetch=1,
        grid=(DB, n_pages // pages),
        in_specs=[
            pl.BlockSpec((None, H, HEAD_DIM), lambda b, p, pt: (b, 0, 0)),
            pl.BlockSpec((chunks * H, LANES), lambda b, p, pt: (0, 0)),
        ] + [page_spec(s) for s in range(pages)] * 2,
        out_specs=pl.BlockSpec((None, H, HEAD_DIM), lambda b, p, pt: (b, 0, 0)),
        scratch_shapes=[pltpu.VMEM((H, HEAD_DIM), F32), pltpu.VMEM((H, LANES), F32)],
    )
    return pl.pallas_call(
        functools.partial(_sb_decode_kernel, pages=pages, heads=H),
        grid_spec=grid_spec,
        out_shape=jax.ShapeDtypeStruct((DB, H, HEAD_DIM), F32),
        compiler_params=_compiler_params(2, pipelined, resident),
    )(page_table, q, jnp.tile(bias[:, None] * LOG2_E, (chunks, LANES)), *([flat_k] * pages), *([flat_v] * pages))


def _t5_bucket(dist):
    dist = np.asarray(dist, np.int64)
    max_exact = NUM_BUCKETS // 2
    large = max_exact + (np.log(np.maximum(dist, 1) / max_exact) / np.log(BUCKET_MAX_DIST / max_exact)
                         * (NUM_BUCKETS - max_exact)).astype(np.int64)
    large = np.minimum(large, NUM_BUCKETS - 1)
    return np.where(dist < max_exact, dist, large).astype(np.int32)


def _dil_offset_bias(rel_bias, g, heads):
    buckets = _t5_bucket(DIL_RATES[g] * np.arange(DIL_STEPS + 1))
    return rel_bias[g * heads:(g + 1) * heads][:, buckets].astype(F32)


def _dil_prompt_bias(offset_bias):
    heads = offset_bias.shape[0]
    S = DIL_STEPS
    by_shift = jnp.concatenate([jnp.full((heads, S - 1), MASK_VALUE, F32), offset_bias[:, ::-1],
                                jnp.full((heads, S), MASK_VALUE, F32)], axis=1)
    skew = jnp.tile(by_shift, (1, S))[:, :S * (3 * S - 1)].reshape(heads, S, 3 * S - 1)
    general = skew[:, :, S - 1:3 * S - 1]
    first = jnp.where(np.arange(2 * S)[None, None, :] >= S, general, MASK_VALUE)
    return jnp.stack([first, general])


def _dil_attn_kernel(q_ref, kc_ref, kp_ref, vc_ref, vp_ref, bias_ref, o_ref, l_ref, *, rate, unroll):
    not_first = jnp.minimum(pl.program_id(1), 1)
    heads = q_ref.shape[0]
    head0 = pl.program_id(2) * heads

    def body(t, c):
        tiles = []
        for u in range(unroll):
            r = t * unroll + u
            rows = pl.ds(r, DIL_STEPS, stride=rate) if rate > 1 else pl.ds(0, DIL_STEPS)
            tiles += [(h, rows) for h in range(heads)]
        scores = []
        for h, rows in tiles:
            keys = jnp.concatenate([kp_ref[h, rows, :], kc_ref[h, rows, :]], axis=0).astype(BF16)
            scores.append(lax.dot_general(q_ref[h, rows, :].astype(BF16), keys, NT_DIMS,
                                          preferred_element_type=F32))
        probs = []
        for (h, rows), s in zip(tiles, scores):
            s = s * ATTN_SCALE + bias_ref[not_first, head0 + h]
            m = jnp.max(s, axis=1, keepdims=True)
            p = jnp.exp(s - m)
            l = jnp.sum(p, axis=1, keepdims=True)
            l_ref[h, rows, :] = jnp.broadcast_to(m + jnp.log(l), (DIL_STEPS, HEAD_DIM))
            probs.append((p.astype(BF16), l))
        for (h, rows), (p, l) in zip(tiles, probs):
            vals = jnp.concatenate([vp_ref[h, rows, :], vc_ref[h, rows, :]], axis=0).astype(BF16)
            o_ref[h, rows, :] = jnp.dot(p, vals, preferred_element_type=F32) / l
        return c

    lax.fori_loop(0, rate // unroll, body, 0)


def _dil_group_attention(qkv, g, bias, B, S, *, block_bytes=2 * 2**20):
    n_slabs, M, _ = qkv.shape
    H = n_slabs // (3 * N_DIL_GROUPS)
    d = DIL_RATES[g]
    chunk = DIL_STEPS * d
    assert S % chunk == 0, "sequence must be a whole number of dilation chunks"
    nc = S // chunk
    hb = _pick(H, max(block_bytes // _nbytes((chunk, HEAD_DIM), F32), 1))
    blk = (hb, chunk, HEAD_DIM)

    def spec(slab, prev):
        base = slab * (H // hb)
        if prev:
            return pl.BlockSpec(blk, lambda b, c, h: (base + h, b * nc + jnp.maximum(c - 1, 0), 0))
        return pl.BlockSpec(blk, lambda b, c, h: (base + h, b * nc + c, 0))

    out_spec = pl.BlockSpec(blk, lambda b, c, h: (h, b * nc + c, 0))
    out_sds = jax.ShapeDtypeStruct((H, M, HEAD_DIM), F32)
    unroll = max(1, min(d, 8 // hb))
    pipelined = 7 * _nbytes(blk, F32)
    resident = 2 * _nbytes(bias.shape, F32) + 16 * unroll * hb * _nbytes((DIL_STEPS, 2 * DIL_STEPS), F32)
    return pl.pallas_call(
        functools.partial(_dil_attn_kernel, rate=d, unroll=unroll),
        grid=(B, nc, H // hb),
        in_specs=[
            spec(3 * g, False), spec(3 * g + 1, False), spec(3 * g + 1, True),
            spec(3 * g + 2, False), spec(3 * g + 2, True),
            pl.BlockSpec(bias.shape, lambda b, c, h: (0, 0, 0, 0)),
        ],
        out_specs=[out_spec, out_spec],
        out_shape=[out_sds, out_sds],
        compiler_params=_compiler_params(3, pipelined, resident),
    )(qkv, qkv, qkv, qkv, qkv, bias)


def _merge_groups(outs, lses):
    m = functools.reduce(jnp.maximum, lses)
    es = [jnp.exp(l - m) for l in lses]
    den = functools.reduce(jnp.add, es)
    num = functools.reduce(jnp.add, [e * o for e, o in zip(es, outs)])
    return num / den


def _dil_merge_outproj_kernel(*refs):
    G = N_DIL_GROUPS
    o_refs, l_refs = refs[:G], refs[G:2 * G]
    w_ref, r_ref, out_ref, mg_ref = refs[2 * G:]

    n = pl.program_id(1)
    chunks = _row_chunks(mg_ref.shape[0])

    def project(rs):
        out_ref[rs, :] = r_ref[rs, :] + jnp.dot(mg_ref[rs, :], w_ref[...], preferred_element_type=F32)

    @pl.when(n == 0)
    def _():
        for rs in chunks:
            for h in range(o_refs[0].shape[0]):
                merged = _merge_groups([o[h, rs, :] for o in o_refs], [l[h, rs, :] for l in l_refs])
                mg_ref[rs, h * HEAD_DIM:(h + 1) * HEAD_DIM] = merged.astype(BF16)
            project(rs)

    @pl.when(n > 0)
    def _():
        for rs in chunks:
            project(rs)


def _dil_merge_outproj(outs, lses, w, layer, res, *, bm_pref=512, bn_pref=1024):
    H, M, _ = outs[0].shape
    K = H * HEAD_DIM
    N = w.shape[2]
    bm = _pick(M, bm_pref)
    bn = _pick(N, bn_pref)
    row_spec = pl.BlockSpec((H, bm, HEAD_DIM), lambda i, n: (0, i, 0))
    pipelined = 2 * N_DIL_GROUPS * _nbytes((bm, K), F32) + _nbytes((K, bn), BF16) + 2 * _nbytes((bm, bn), F32)
    resident = _nbytes((bm, K), BF16) + 6 * _nbytes((bm, K), F32)
    return pl.pallas_call(
        _dil_merge_outproj_kernel,
        grid=(M // bm, N // bn),
        in_specs=[row_spec] * (2 * N_DIL_GROUPS) + [
            _weight_spec(layer, K, bn),
            pl.BlockSpec((bm, bn), lambda i, n: (i, n)),
        ],
        out_specs=pl.BlockSpec((bm, bn), lambda i, n: (i, n)),
        out_shape=jax.ShapeDtypeStruct((M, N), F32),
        scratch_shapes=[pltpu.VMEM((bm, K), BF16)],
        compiler_params=_compiler_params(2, pipelined, resident),
    )(*outs, *lses, w, res)


def _dil_decode_kernel(qkv_ref, w0_ref, w1_ref, w2_ref, wb_ref, sb_ref, o_ref):
    Hd = o_ref.shape[0]
    outs, lses = [], []
    for g, win_ref in enumerate((w0_ref, w1_ref, w2_ref)):
        flat = win_ref[...].reshape(DIL_STEPS * 2 * Hd, HEAD_DIM).astype(BF16)
        q = qkv_ref[3 * g]
        k_new = qkv_ref[3 * g + 1]
        v_new = qkv_ref[3 * g + 2]
        s_win = lax.dot_general(q.astype(BF16), flat, NT_DIMS, preferred_element_type=F32) * ATTN_SCALE
        s_win = s_win + wb_ref[g]
        s_new = jnp.sum(q * k_new, axis=1, keepdims=True) * ATTN_SCALE + sb_ref[g][:, :1]
        m = jnp.maximum(jnp.max(s_win, axis=1, keepdims=True), s_new)
        p_win = jnp.exp(s_win - m)
        p_new = jnp.exp(s_new - m)
        l = jnp.sum(p_win, axis=1, keepdims=True) + p_new
        p_on_v = pltpu.roll(p_win, Hd, axis=1)
        pv = jnp.dot(p_on_v.astype(BF16), flat, preferred_element_type=F32) + p_new * v_new
        outs.append(pv / l)
        lses.append(m + jnp.log(l))
    o_ref[...] = _merge_groups(outs, lses)


def _dil_decode(qkv, states, layer, win_bias, self_bias):
    DB, n_slabs, Hd, _ = qkv.shape
    views = []
    for g, st in enumerate(states):
        L = st.shape[2]
        assert L == DIL_WINDOWS[g], "window state must hold exactly one window"
        d = DIL_RATES[g]
        views.append(st.reshape(st.shape[0], DB, L // d, d * 2 * Hd, HEAD_DIM))
    win_spec = pl.BlockSpec((None, None, DIL_STEPS, 2 * Hd, HEAD_DIM), lambda b: (layer, b, 0, 0, 0))
    pipelined = 3 * _nbytes((DIL_STEPS, 2 * Hd, HEAD_DIM), F32) + _nbytes((n_slabs, Hd, HEAD_DIM), F32)
    resident = (3 * _nbytes((DIL_STEPS, 2 * Hd, HEAD_DIM), F32) + 2 * _nbytes(win_bias.shape, F32)
                + 24 * _nbytes((Hd, DIL_STEPS * 2 * Hd), F32))
    return pl.pallas_call(
        _dil_decode_kernel,
        grid=(DB,),
        in_specs=[
            pl.BlockSpec((None, n_slabs, Hd, HEAD_DIM), lambda b: (b, 0, 0, 0)),
            win_spec, win_spec, win_spec,
            pl.BlockSpec(win_bias.shape, lambda b: (0, 0, 0)),
            pl.BlockSpec(self_bias.shape, lambda b: (0, 0, 0)),
        ],
        out_specs=pl.BlockSpec((None, Hd, HEAD_DIM), lambda b: (b, 0, 0)),
        out_shape=jax.ShapeDtypeStruct((DB, Hd, HEAD_DIM), F32),
        compiler_params=_compiler_params(1, pipelined, resident),
    )(qkv, *views, win_bias, self_bias)


def _dil_decode_bias(offset_bias):
    heads = offset_bias.shape[0]
    own_k = np.arange(2 * heads)[None, :] == np.arange(heads)[:, None]
    table = jnp.where(own_k[:, None, :], offset_bias[:, :0:-1, None], MASK_VALUE)
    return table.reshape(heads, -1), jnp.broadcast_to(offset_bias[:, :1], (heads, LANES))


def _window_shift_kernel(cur_ref, nxt_ref, new_ref, o_ref):
    R = o_ref.shape[0]
    last = pl.program_id(1) == pl.num_programs(1) - 1
    if R > 1:
        o_ref[0:R - 1] = cur_ref[1:R]
    o_ref[R - 1] = jnp.where(last, new_ref[0], nxt_ref[0])


def _window_shift(state, new_rows, *, block_bytes=2 * 2**20):
    n_layers, DB, L, two, Hd, _ = state.shape
    row = (two * Hd, HEAD_DIM)
    R = _pick(L, max(block_bytes // _nbytes(row, F32), 1))
    flat = state.reshape(n_layers * DB, L, *row)
    out = pl.pallas_call(
        _window_shift_kernel,
        grid=(n_layers * DB, L // R),
        in_specs=[
            pl.BlockSpec((None, R, *row), lambda s, c: (s, c, 0, 0)),
            pl.BlockSpec((None, 1, *row), lambda s, c: (s, jnp.minimum((c + 1) * R, L - 1), 0, 0)),
            pl.BlockSpec((None, 1, *row), lambda s, c: (s, 0, 0, 0)),
        ],
        out_specs=pl.BlockSpec((None, R, *row), lambda s, c: (s, c, 0, 0)),
        out_shape=jax.ShapeDtypeStruct(flat.shape, state.dtype),
        compiler_params=_compiler_params(2, 2 * _nbytes((R, *row), F32), 0),
    )(flat, flat, new_rows.reshape(n_layers * DB, 1, *row))
    return out.reshape(state.shape)


def _pad_rows(x, rows):
    return jnp.pad(x, ((0, rows - x.shape[0]), (0, 0)))


def kernel(x_prompt, x_sample, cache_sb_k, cache_sb_v, state_win0_kv, state_win1_kv, state_win2_kv, page_table, p_prompt, p_sample, g_mix, g_ffn, g_ple, w_qkv_sb, w_o_sb, b_sb, w_qkv_dil, w_o_dil, g_qnorm_dil, g_knorm_dil, rel_bias, w_ffn_gate, w_ffn_up, w_ffn_down, w_ple_proj, w_ple_gate):
    B, S, D = x_prompt.shape
    DB, T, _ = x_sample.shape
    assert T == 1, "decode step handles one new token per sample row"
    depth = g_mix.shape[0]
    M = B * S
    sb_heads = D // HEAD_DIM
    dil_w = w_o_dil.shape[1]
    dil_heads = dil_w // HEAD_DIM
    win_states = (state_win0_kv, state_win1_kv, state_win2_kv)

    hp = x_prompt.reshape(M, D)
    hs = _pad_rows(x_sample.reshape(DB, D), SAMPLE_ROWS)

    w_qkv_sb, w_o_sb, w_qkv_dil, w_o_dil, w_ffn_gate, w_ffn_up, w_ffn_down, w_ple_proj, w_ple_gate = [
        _cast_bf16(w) for w in (w_qkv_sb, w_o_sb, w_qkv_dil, w_o_dil, w_ffn_gate, w_ffn_up, w_ffn_down,
                                w_ple_proj, w_ple_gate)]

    offset_bias = [_dil_offset_bias(rel_bias, g, dil_heads) for g in range(N_DIL_GROUPS)]
    prompt_bias = [_dil_prompt_bias(ob) for ob in offset_bias]
    win_bias, self_bias = (jnp.stack(t) for t in zip(*[_dil_decode_bias(ob) for ob in offset_bias]))

    n_sb = w_qkv_sb.shape[0]
    sb_prompt_kv = [jnp.zeros((n_sb, M, sb_heads, HEAD_DIM), F32) for _ in range(2)]
    sb_sample_kv = [jnp.zeros((n_sb, SAMPLE_ROWS, sb_heads, HEAD_DIM), F32) for _ in range(2)]
    win_p = [[] for _ in range(N_DIL_GROUPS)]
    win_s = [[] for _ in range(N_DIL_GROUPS)]

    for i in range(depth):
        j = i // 2
        if i % 2 == 0:
            k_p, v_p, qkv = _sb_qkv(hp, g_mix[i], w_qkv_sb, j, sb_prompt_kv)
            sb_prompt_kv = (k_p, v_p)
            attn = _sb_attention(qkv, b_sb[j], B, S)
            hp = _matmul_res(attn, w_o_sb, j, hp, bm_pref=1024, bn_pref=1024)

            k_s, v_s, qkv_s = _sb_qkv(hs, g_mix[i], w_qkv_sb, j, sb_sample_kv)
            sb_sample_kv = (k_s, v_s)
            q_s = qkv_s[0, :DB].astype(F32).reshape(DB, sb_heads, HEAD_DIM)
            attn_s = _sb_decode(q_s, cache_sb_k, cache_sb_v, j, page_table, b_sb[j])
            hs = _matmul_res(_pad_rows(attn_s.reshape(DB, D), SAMPLE_ROWS), w_o_sb, j, hs)
        else:
            ones = jnp.ones((HEAD_DIM,), F32)
            head_gain = jnp.stack([jnp.tile(gain, dil_heads)
                                   for g in range(N_DIL_GROUPS)
                                   for gain in (g_qnorm_dil[j, g], g_knorm_dil[j, g], ones)])[:, None, :]
            qkv = _dil_qkv(hp, g_mix[i], w_qkv_dil, j, head_gain)
            outs, lses = zip(*[_dil_group_attention(qkv, g, prompt_bias[g], B, S)
                               for g in range(N_DIL_GROUPS)])
            hp = _dil_merge_outproj(outs, lses, w_o_dil, j, hp)
            kv6 = qkv.reshape(N_DIL_GROUPS, 3, dil_heads, B, S, HEAD_DIM)
            for g in range(N_DIL_GROUPS):
                L = min(DIL_WINDOWS[g], S)
                win_p[g].append(jnp.transpose(kv6[g, 1:3, :, :, S - L:], (2, 3, 0, 1, 4)))

            qkv_s = _dil_qkv(hs, g_mix[i], w_qkv_dil, j, head_gain)[:, :DB]
            qkv_s = jnp.transpose(qkv_s.reshape(3 * N_DIL_GROUPS, dil_heads, DB, HEAD_DIM), (2, 0, 1, 3))
            attn_s = _dil_decode(qkv_s, win_states, j, win_bias, self_bias)
            hs = _matmul_res(_pad_rows(attn_s.reshape(DB, dil_w), SAMPLE_ROWS), w_o_dil, j, hs)
            for g in range(N_DIL_GROUPS):
                win_s[g].append(qkv_s[:, None, 3 * g + 1:3 * g + 3])

        hp = _matmul_res(_ffn_up(hp, g_ffn[i], w_ffn_gate, w_ffn_up, i), w_ffn_down, i, hp, bm_pref=1024)
        hp = _ple(hp, g_ple[i], w_ple_gate, p_prompt[i].reshape(M, -1), w_ple_proj, i)
        hs = _matmul_res(_ffn_up(hs, g_ffn[i], w_ffn_gate, w_ffn_up, i), w_ffn_down, i, hs)
        hs = _ple(hs, g_ple[i], w_ple_gate, _pad_rows(p_sample[i].reshape(DB, -1), SAMPLE_ROWS), w_ple_proj, i)

    new_states = [_window_shift(st, jnp.stack(rows)) for st, rows in zip(win_states, win_s)]
    return (hp.reshape(B, S, D), hs[:DB].reshape(DB, 1, D),
            *[t.reshape(-1, B, S, sb_heads, HEAD_DIM) for t in sb_prompt_kv],
            *[t[:, :DB].reshape(-1, DB, 1, sb_heads, HEAD_DIM) for t in sb_sample_kv],
            jnp.stack(win_p[0]), jnp.stack(win_p[1]), jnp.stack(win_p[2]), *new_states)
```
